```python
import math
import jax
import jax.numpy as jnp
from jax import lax
import numpy as np

D_MODEL = 1024
BATCH = 8
SEQ = 4096
DEPTH = 2

HEAD_DIM = 64
N_MOBA_HEADS = 4
MOBA_BLOCK = 256
MOBA_TOPK = 3
MOBA_Q_CHUNK = 32
N_DIFF_HEADS = 4
DIFF_HEAD_DIM = 32
DIFF_Q_BLOCK = 128
N_POOL_GROUPS = 4
POOL_GROUP = 64
POOL_WINDOWS = (2, 4, 8, 16)
N_DSA_HEADS = 4
N_IDX_HEADS = 4
IDX_HEAD_DIM = 32
DSA_TOPK = 256
DSA_Q_CHUNK = 64
N_BRANCHES = 4
BRANCH_WIDTH = 256
D_FF = 4 * D_MODEL
N_ALIBI_HEADS = N_MOBA_HEADS + N_DIFF_HEADS + N_DSA_HEADS
NORM_EPS = 1e-6
NEG_INF = -1e30

MOBA_W = N_MOBA_HEADS * HEAD_DIM
DIFF_W = N_DIFF_HEADS * 2 * DIFF_HEAD_DIM
POOL_W = N_POOL_GROUPS * POOL_GROUP
DSA_W = N_DSA_HEADS * HEAD_DIM
IDX_Q_W = N_IDX_HEADS * IDX_HEAD_DIM
IN_SPLITS = (MOBA_W, MOBA_W, MOBA_W, DIFF_W, DIFF_W, DIFF_W, POOL_W,
             DSA_W, DSA_W, DSA_W, IDX_Q_W, IDX_HEAD_DIM, N_IDX_HEADS)
IN_PROJ_WIDTH = 3 * MOBA_W + 3 * DIFF_W + POOL_W + 3 * DSA_W + IDX_Q_W + IDX_HEAD_DIM + N_IDX_HEADS

kernel_name = 'hybrid_gated_moba_diff_pool_dsa'


def rms_norm(x, g):
    xf = x.astype(jnp.float32)
    y = xf * lax.rsqrt(jnp.mean(xf * xf, axis=-1, keepdims=True) + NORM_EPS)
    return (y * g.astype(jnp.float32)).astype(x.dtype)


def alibi_slopes():
    n = N_ALIBI_HEADS
    return jnp.asarray(2.0 ** (-8.0 * np.arange(1, n + 1) / n), dtype=jnp.float32)


def _chunks(fn, seq, size):
    starts = jnp.arange(seq // size, dtype=jnp.int32) * size
    out = jnp.moveaxis(lax.map(fn, starts), 0, 1)
    return out.reshape((out.shape[0], seq) + out.shape[3:])


def moba_attention(q, k, v, slopes):
    B, S, H, Dh = q.shape
    f32 = jnp.float32
    q, k, v = q.astype(f32), k.astype(f32), v.astype(f32)
    n_blk = -(-S // MOBA_BLOCK)
    pad = ((0, 0), (0, n_blk * MOBA_BLOCK - S), (0, 0), (0, 0))
    kp, vp = jnp.pad(k, pad), jnp.pad(v, pad)
    kb = kp.reshape(B, n_blk, MOBA_BLOCK, H, Dh).transpose(0, 3, 1, 2, 4)
    vb = vp.reshape(B, n_blk, MOBA_BLOCK, H, Dh).transpose(0, 3, 1, 2, 4)
    k_mean = kb.mean(axis=3)
    n_sel = min(MOBA_TOPK, n_blk)
    scale = Dh ** -0.5
    blk_ids = jnp.arange(n_blk)
    offs = jnp.arange(MOBA_BLOCK)
    b_ix = jnp.arange(B)[:, None, None, None]
    h_ix = jnp.arange(H)[None, None, :, None]
    m = slopes[None, None, :, None]

    def chunk(start):
        qc = lax.dynamic_slice_in_dim(q, start, MOBA_Q_CHUNK, axis=1)
        t = start + jnp.arange(MOBA_Q_CHUNK)
        cur = start // MOBA_BLOCK
        gate = jnp.einsum('bqhd,bhnd->bqhn', qc, k_mean)
        gate = jnp.where(blk_ids < cur, gate, NEG_INF)
        _, sel = lax.top_k(gate, n_sel)
        kg = kb[b_ix, h_ix, sel]
        vg = vb[b_ix, h_ix, sel]
        s_sel = jnp.einsum('bqhd,bqhnkd->bqhnk', qc, kg) * scale
        pos_sel = sel[..., None] * MOBA_BLOCK + offs
        dist_sel = (t[None, :, None, None, None] - pos_sel).astype(f32)
        s_sel = jnp.where((sel < cur)[..., None], s_sel - m[..., None] * dist_sel, NEG_INF)
        k_own = lax.dynamic_slice_in_dim(kp, cur * MOBA_BLOCK, MOBA_BLOCK, axis=1)
        v_own = lax.dynamic_slice_in_dim(vp, cur * MOBA_BLOCK, MOBA_BLOCK, axis=1)
        s_own = jnp.einsum('bqhd,bkhd->bqhk', qc, k_own) * scale
        dist_own = (t[:, None] - (cur * MOBA_BLOCK + offs)[None, :]).astype(f32)[None, :, None, :]
        s_own = jnp.where(dist_own >= 0, s_own - m * dist_own, NEG_INF)
        s_all = jnp.concatenate([s_sel.reshape(B, MOBA_Q_CHUNK, H, n_sel * MOBA_BLOCK), s_own], axis=-1)
        p = jax.nn.softmax(s_all, axis=-1)
        p_sel = p[..., :n_sel * MOBA_BLOCK].reshape(B, MOBA_Q_CHUNK, H, n_sel, MOBA_BLOCK)
        p_own = p[..., n_sel * MOBA_BLOCK:]
        return (jnp.einsum('bqhnk,bqhnkd->bqhd', p_sel, vg)
                + jnp.einsum('bqhk,bkhd->bqhd', p_own, v_own))

    return _chunks(chunk, S, MOBA_Q_CHUNK)


def diff_attention(q, k, v, slopes, lam, lambda_init, subln_g):
    B, S, H, _, dh = q.shape
    f32 = jnp.float32
    q, k, v = q.astype(f32), k.astype(f32), v.astype(f32)
    scale = dh ** -0.5
    s_pos = jnp.arange(S)
    m = slopes[:, None, None, None]

    def block(start):
        qb = lax.dynamic_slice_in_dim(q, start, DIFF_Q_BLOCK, axis=1)
        t = start + jnp.arange(DIFF_Q_BLOCK)
        dist = (t[:, None] - s_pos[None, :]).astype(f32)
        s = jnp.einsum('bqhmd,bshmd->bhmqs', qb, k) * scale
        s = jnp.where(dist >= 0, s - m * dist, NEG_INF)
        p = jax.nn.softmax(s, axis=-1)
        a = p[:, :, 0] - lam * p[:, :, 1]
        return jnp.einsum('bhqs,bshe->bqhe', a, v)

    o = _chunks(block, S, DIFF_Q_BLOCK)
    o = o * lax.rsqrt(jnp.mean(o * o, axis=-1, keepdims=True) + NORM_EPS) * subln_g
    return o * (1.0 - lambda_init)


def pool_mixer(u, w_pool, pool_scale):
    B, S, _ = u.shape
    uf = u.astype(jnp.float32)
    cs = jnp.cumsum(uf, axis=1)
    t = jnp.arange(S)
    outs = []
    for g, w in enumerate(POOL_WINDOWS):
        c_g = cs[..., g * POOL_GROUP:(g + 1) * POOL_GROUP]
        lower = jnp.pad(c_g[:, :S - w], ((0, 0), (w, 0), (0, 0)))
        cnt = jnp.minimum(t + 1, w).astype(jnp.float32)[None, :, None]
        outs.append((c_g - lower) / cnt - uf[..., g * POOL_GROUP:(g + 1) * POOL_GROUP])
    d = jnp.stack(outs, axis=2)
    y = jnp.einsum('bsgc,gce->bsge', d, w_pool).reshape(B, S, POOL_W)
    return (y * pool_scale).astype(u.dtype)


def dsa_attention(q, k, v, iq, ik, iw, slopes):
    B, S, H, Dh = q.shape
    f32 = jnp.float32
    q, k, v = q.astype(f32), k.astype(f32), v.astype(f32)
    iq, ik = iq.astype(f32), ik.astype(f32)
    iw = iw.astype(f32) * N_IDX_HEADS ** -0.5
    n_sel = min(DSA_TOPK, S // 4)
    s_pos = jnp.arange(S)
    b_ix = jnp.arange(B)[:, None, None]
    m = slopes[None, None, :, None]

    def chunk(start):
        qc = lax.dynamic_slice_in_dim(q, start, DSA_Q_CHUNK, axis=1)
        iqc = lax.dynamic_slice_in_dim(iq, start, DSA_Q_CHUNK, axis=1)
        iwc = lax.dynamic_slice_in_dim(iw, start, DSA_Q_CHUNK, axis=1)
        t = start + jnp.arange(DSA_Q_CHUNK)
        rel = jax.nn.relu(jnp.einsum('bqhd,bsd->bqhs', iqc, ik) * IDX_HEAD_DIM ** -0.5)
        score = jnp.einsum('bqhs,bqh->bqs', rel, iwc)
        score = jnp.where(s_pos[None, None, :] <= t[None, :, None], score, NEG_INF)
        _, sel = lax.top_k(score, n_sel)
        kg = k[b_ix, sel]
        vg = v[b_ix, sel]
        dist = (t[None, :, None] - sel).astype(f32)[:, :, None, :]
        s = jnp.einsum('bqhd,bqnhd->bqhn', qc, kg) * Dh ** -0.5
        s = jnp.where(dist >= 0, s - m * dist, NEG_INF)
        p = jax.nn.softmax(s, axis=-1)
        return jnp.einsum('bqhn,bqnhd->bqhd', p, vg)

    return _chunks(chunk, S, DSA_Q_CHUNK)


def hybrid_layer(x, c_act, layer, slopes, w_ada, b_ada, norm1_g, w_in, diff_lambda, diff_subln_g,
                 w_pool, pool_scale, w_gate, b_gate, w_branch, w_out, norm2_g, w_ff1, w_ff2):
    B, S, D = x.shape
    f32 = jnp.float32
    mod = (c_act @ w_ada + b_ada)[:, None, :]
    sh1, sc1, g1, sh2, sc2, g2 = jnp.split(mod, 6, axis=-1)
    h = rms_norm(x, norm1_g) * (1.0 + sc1) + sh1
    proj = h @ w_in
    offsets = np.cumsum(IN_SPLITS)[:-1].tolist()
    aq, ak, av, bq, bk, bv, pu, dq, dk, dv, iq, ik, iw = jnp.split(proj, offsets, axis=-1)
    slopes_a, slopes_b, slopes_d = slopes[0::3], slopes[1::3], slopes[2::3]

    y_a = moba_attention(aq.reshape(B, S, N_MOBA_HEADS, HEAD_DIM), ak.reshape(B, S, N_MOBA_HEADS, HEAD_DIM),
                         av.reshape(B, S, N_MOBA_HEADS, HEAD_DIM), slopes_a).reshape(B, S, MOBA_W)

    lambda_init = 0.8 - 0.6 * math.exp(-0.3 * layer)
    lq1, lk1, lq2, lk2 = diff_lambda.astype(f32)
    lam = jnp.exp(jnp.sum(lq1 * lk1)) - jnp.exp(jnp.sum(lq2 * lk2)) + lambda_init
    qk_shape = (B, S, N_DIFF_HEADS, 2, DIFF_HEAD_DIM)
    y_b = diff_attention(bq.reshape(qk_shape), bk.reshape(qk_shape),
                         bv.reshape(B, S, N_DIFF_HEADS, 2 * DIFF_HEAD_DIM), slopes_b, lam, lambda_init,
                         diff_subln_g.astype(f32)).reshape(B, S, DIFF_W)

    y_c = pool_mixer(pu, w_pool, pool_scale)

    y_d = dsa_attention(dq.reshape(B, S, N_DSA_HEADS, HEAD_DIM), dk.reshape(B, S, N_DSA_HEADS, HEAD_DIM),
                        dv.reshape(B, S, N_DSA_HEADS, HEAD_DIM), iq.reshape(B, S, N_IDX_HEADS, IDX_HEAD_DIM),
                        ik, iw, slopes_d).reshape(B, S, DSA_W)

    terms = []
    for i, y in enumerate((y_a, y_b, y_c, y_d)):
        gate = jax.nn.sigmoid((h @ w_gate[i] + b_gate[i]).astype(f32))
        terms.append(gate * (y.astype(x.dtype) @ w_branch[i]))
    merged = (terms[0] + terms[1] + terms[2] + terms[3]).astype(x.dtype)
    x = x + g1 * (merged @ w_out)

    h2 = rms_norm(x, norm2_g) * (1.0 + sc2) + sh2
    ff = jnp.square(jax.nn.relu(h2 @ w_ff1)) @ w_ff2
    return x + g2 * ff


def setup_inputs(seed: int = 0) -> dict:
    key = jax.random.key(seed)
    ks = jax.random.split(key, 20)
    D = D_MODEL

    def nrm(k, shape, scale):
        return jax.random.normal(k, shape, jnp.float32) * scale

    return {
        'x': nrm(ks[0], (BATCH, SEQ, D), 1.0),
        'c': nrm(ks[1], (BATCH, D), 1.0),
        'w_ada': nrm(ks[2], (DEPTH, D, 6 * D), 0.5 * D ** -0.5),
        'b_ada': nrm(ks[3], (DEPTH, 6 * D), 0.02),
        'norm1_g': 1.0 + nrm(ks[4], (DEPTH, D), 0.02),
        'w_in': nrm(ks[5], (DEPTH, D, IN_PROJ_WIDTH), D ** -0.5),
        'diff_lambda': nrm(ks[6], (DEPTH, 4, DIFF_HEAD_DIM), 0.1),
        'diff_subln_g': 1.0 + nrm(ks[7], (DEPTH, 2 * DIFF_HEAD_DIM), 0.02),
        'w_pool': nrm(ks[8], (DEPTH, N_POOL_GROUPS, POOL_GROUP, POOL_GROUP), POOL_GROUP ** -0.5),
        'pool_scale': 1.0 + nrm(ks[9], (DEPTH, POOL_W), 0.1),
        'w_gate': nrm(ks[10], (DEPTH, N_BRANCHES, D, D), D ** -0.5),
        'b_gate': nrm(ks[11], (DEPTH, N_BRANCHES, D), 0.02),
        'w_branch': nrm(ks[12], (DEPTH, N_BRANCHES, BRANCH_WIDTH, D), BRANCH_WIDTH ** -0.5),
        'w_out': nrm(ks[13], (DEPTH, D, D), D ** -0.5),
        'norm2_g': 1.0 + nrm(ks[14], (DEPTH, D), 0.02),
        'w_ff1': nrm(ks[15], (DEPTH, D, D_FF), D ** -0.5),
        'w_ff2': nrm(ks[16], (DEPTH, D_FF, D), D_FF ** -0.5),
        'final_norm_g': 1.0 + nrm(ks[17], (D,), 0.02),
    }


def reference(x, c, w_ada, b_ada, norm1_g, w_in, diff_lambda, diff_subln_g, w_pool, pool_scale,
              w_gate, b_gate, w_branch, w_out, norm2_g, w_ff1, w_ff2, final_norm_g):
    slopes = alibi_slopes()
    c_act = jax.nn.silu(c)
    for l in range(DEPTH):
        x = hybrid_layer(x, c_act, l, slopes, w_ada[l], b_ada[l], norm1_g[l], w_in[l], diff_lambda[l],
                         diff_subln_g[l], w_pool[l], pool_scale[l], w_gate[l], b_gate[l], w_branch[l],
                         w_out[l], norm2_g[l], w_ff1[l], w_ff2[l])
    return rms_norm(x, final_norm_g)
```

```python
import functools
import math

import numpy as np
import jax
import jax.numpy as jnp
from jax import lax
from jax.experimental import pallas as pl
from jax.experimental.pallas import tpu as pltpu

F32 = jnp.float32
BF16 = jnp.bfloat16
HIGHEST = lax.Precision.HIGHEST

NEG = -1e30
BELOW_NEG = -3.0e38
NORM_EPS = 1e-6
BLK = 256
LANE = 128
HEAD = 64
N_HEADS = 4
DIFF_DH = 32
MOBA_TOPK = 3
DSA_TOPK = 256
IDX_DH = 32
N_IDX_HEADS = 4
POOL_WINDOWS = (2, 4, 8, 16)
POOL_HALO = 16
INT_MIN = -(2 ** 31)
VMEM_LIMIT = 56 * 1024 * 1024

_ALL_SLOPES = [float(np.float32(2.0 ** (-8.0 * i / 12))) for i in range(1, 13)]
SLOPES_A = _ALL_SLOPES[0::3]
SLOPES_B = _ALL_SLOPES[1::3]
SLOPES_D = _ALL_SLOPES[2::3]

ATT_W = 9 * 256
PU_OFF = ATT_W
IDX_OFF = ATT_W + 256
IN_W = IDX_OFF + 3 * LANE


def _key_of_neg():
    bits = int(np.array(NEG, np.float32).view(np.int32))
    return INT_MIN - bits


KEY_NEG = _key_of_neg()


def _nt(a, b, precision=None):
    return lax.dot_general(a, b, (((1,), (1,)), ((), ())), preferred_element_type=F32, precision=precision)


def _modulated_norm(x, g, shift, scale):
    ms = jnp.mean(x * x, axis=-1, keepdims=True)
    y = x * lax.rsqrt(ms + NORM_EPS) * g
    return y * (1.0 + scale) + shift


def _mod_kernel(c_ref, w_ref, b_ref, o_ref):
    c = c_ref[...]
    act = c * jax.nn.sigmoid(c)
    o_ref[0] = jnp.dot(act, w_ref[0], preferred_element_type=F32, precision=HIGHEST) + b_ref[0]


def _modulation(c, w_ada, b_ada):
    depth, d, n = w_ada.shape
    b = c.shape[0]
    tn = n // 4
    return pl.pallas_call(
        _mod_kernel,
        out_shape=jax.ShapeDtypeStruct((depth, b, n), F32),
        grid=(depth, n // tn),
        in_specs=[pl.BlockSpec((b, d), lambda l, j: (0, 0)),
                  pl.BlockSpec((1, d, tn), lambda l, j: (l, 0, j)),
                  pl.BlockSpec((1, 1, tn), lambda l, j: (l, 0, j))],
        out_specs=pl.BlockSpec((1, b, tn), lambda l, j: (l, 0, j)),
        compiler_params=pltpu.CompilerParams(vmem_limit_bytes=VMEM_LIMIT),
        name="adaln_mod",
    )(c, w_ada, b_ada.reshape(depth, 1, n))


def _in_kernel(x_ref, mod_ref, g_ref, w_ref, cs_ref, att_ref, pu_ref, idx_ref, km_ref):
    mod = mod_ref[0]
    h = _modulated_norm(x_ref[...], g_ref[...], mod[0:1], mod[1:2])
    proj = jnp.dot(h.astype(BF16), w_ref[...], preferred_element_type=F32)
    att_ref[...] = (proj[:, :ATT_W] * cs_ref[...]).astype(BF16)
    pu_ref[...] = proj[:, PU_OFF:PU_OFF + 256]
    idx_ref[...] = proj[:, IDX_OFF:]
    ak = proj[:, 256:512]
    for r in range(km_ref.shape[0]):
        km_ref[r] = jnp.mean(ak[r * BLK:(r + 1) * BLK], axis=0, keepdims=True)


def _in_proj(x2, mod, g, w_in_r, colscale, seq, tm):
    t, d = x2.shape
    nb = tm // BLK
    return pl.pallas_call(
        _in_kernel,
        out_shape=(jax.ShapeDtypeStruct((t, ATT_W), BF16),
                   jax.ShapeDtypeStruct((t, 256), F32),
                   jax.ShapeDtypeStruct((t, 3 * LANE), F32),
                   jax.ShapeDtypeStruct((t // BLK, 1, 256), F32)),
        grid=(t // tm,),
        in_specs=[pl.BlockSpec((tm, d), lambda i: (i, 0)),
                  pl.BlockSpec((1, 6, d), lambda i: (i * tm // seq, 0, 0)),
                  pl.BlockSpec((1, d), lambda i: (0, 0)),
                  pl.BlockSpec((d, IN_W), lambda i: (0, 0)),
                  pl.BlockSpec((1, ATT_W), lambda i: (0, 0))],
        out_specs=(pl.BlockSpec((tm, ATT_W), lambda i: (i, 0)),
                   pl.BlockSpec((tm, 256), lambda i: (i, 0)),
                   pl.BlockSpec((tm, 3 * LANE), lambda i: (i, 0)),
                   pl.BlockSpec((nb, 1, 256), lambda i: (i, 0, 0))),
        compiler_params=pltpu.CompilerParams(vmem_limit_bytes=VMEM_LIMIT),
        name="in_proj",
    )(x2, mod, g, w_in_r, colscale)


def _head_masks(dtype):
    lane = lax.broadcasted_iota(jnp.int32, (BLK, LANE), 1)
    return lane, [lane < HEAD, lane >= HEAD]


def _init_state(m_ref, l_ref, acc_ref):
    m_ref[...] = jnp.full(m_ref.shape, NEG, F32)
    l_ref[...] = jnp.zeros(l_ref.shape, F32)
    acc_ref[...] = jnp.zeros(acc_ref.shape, F32)


def _online_update(s, c, n, v_blk, m_ref, l_ref, acc_ref):
    m_old = m_ref[n]
    m_new = jnp.maximum(m_old, jnp.max(s, axis=1, keepdims=True) + c)
    p = jnp.exp(s - (m_new - c))
    alpha = jnp.exp(m_old - m_new)
    l_ref[n] = alpha * l_ref[n] + jnp.sum(p, axis=1, keepdims=True)
    acc_ref[n] = alpha * acc_ref[n] + jnp.dot(p.astype(BF16), v_blk, preferred_element_type=F32)
    m_ref[n] = m_new


def _block_offset(slope, i, j):
    steps = jnp.full((1, 1), i - j, jnp.int32).astype(F32)
    return steps * (-slope * BLK)


def _local_dist():
    tl = lax.broadcasted_iota(jnp.int32, (BLK, BLK), 0)
    sl = lax.broadcasted_iota(jnp.int32, (BLK, BLK), 1)
    return tl - sl


def _moba_kernel(q_ref, k_ref, v_ref, km_ref, bias_ref, o_ref, m_ref, l_ref, acc_ref, sel_ref):
    i = pl.program_id(1)
    lane, halves = _head_masks(BF16)
    lane_f = lane.astype(F32)
    causal = _local_dist() >= 0
    _init_state(m_ref, l_ref, acc_ref)
    q = q_ref[0]
    own = pl.multiple_of(i * BLK, BLK)
    k_own = k_ref[0, pl.ds(own, BLK), :]
    v_own = v_ref[0, pl.ds(own, BLK), :]
    qms = []
    for h in range(N_HEADS):
        slot, half = divmod(h, 2)
        qm = jnp.where(halves[half], q[:, slot * LANE:(slot + 1) * LANE], jnp.zeros((), BF16))
        qms.append(qm)
        gate = _nt(qm.astype(F32), km_ref[0][:, slot * LANE:(slot + 1) * LANE], HIGHEST)
        g = jnp.where(lane < i, gate, NEG)
        sel = jnp.zeros((BLK, LANE), F32)
        for _ in range(MOBA_TOPK):
            mx = jnp.max(g, axis=1, keepdims=True)
            first = jnp.min(jnp.where(g == mx, lane_f, 1e9), axis=1, keepdims=True)
            pick = lane_f == first
            sel = jnp.where(pick, 1.0, sel)
            g = jnp.where(pick, BELOW_NEG, g)
        sel_ref[h] = jnp.where(lane < i, sel, 0.0)
        s = _nt(qm, k_own[:, slot * LANE:(slot + 1) * LANE]) + bias_ref[h]
        s = jnp.where(causal, s, NEG)
        _online_update(s, 0.0, h, v_own[:, slot * LANE:(slot + 1) * LANE], m_ref, l_ref, acc_ref)

    def body(j, carry):
        start = pl.multiple_of(j * BLK, BLK)
        k_j = k_ref[0, pl.ds(start, BLK), :]
        v_j = v_ref[0, pl.ds(start, BLK), :]
        for h in range(N_HEADS):
            slot, half = divmod(h, 2)
            chosen = jnp.sum(jnp.where(lane == j, sel_ref[h], 0.0), axis=1, keepdims=True)
            s = _nt(qms[h], k_j[:, slot * LANE:(slot + 1) * LANE]) + bias_ref[h]
            s = jnp.where(chosen > 0.5, s, NEG)
            _online_update(s, _block_offset(SLOPES_A[h], i, j), h,
                           v_j[:, slot * LANE:(slot + 1) * LANE], m_ref, l_ref, acc_ref)
        return carry

    lax.fori_loop(0, i, body, 0)
    for slot in range(2):
        o0 = acc_ref[2 * slot] / l_ref[2 * slot]
        o1 = acc_ref[2 * slot + 1] / l_ref[2 * slot + 1]
        o_ref[0, :, slot * LANE:(slot + 1) * LANE] = jnp.where(halves[0], o0, o1).astype(o_ref.dtype)


def _moba(att, kmean_pad, bias, seq):
    b = att.shape[0]
    nq = seq // BLK
    return pl.pallas_call(
        _moba_kernel,
        out_shape=jax.ShapeDtypeStruct((b, seq, 256), BF16),
        grid=(b, nq),
        in_specs=[pl.BlockSpec((1, BLK, 256), lambda bi, i: (bi, i, 0)),
                  pl.BlockSpec((1, seq, 256), lambda bi, i: (bi, 0, 1)),
                  pl.BlockSpec((1, seq, 256), lambda bi, i: (bi, 0, 2)),
                  pl.BlockSpec((1, LANE, 256), lambda bi, i: (bi, 0, 0)),
                  pl.BlockSpec((N_HEADS, BLK, BLK), lambda bi, i: (0, 0, 0))],
        out_specs=pl.BlockSpec((1, BLK, 256), lambda bi, i: (bi, i, 0)),
        scratch_shapes=[pltpu.VMEM((N_HEADS, BLK, 1), F32),
                        pltpu.VMEM((N_HEADS, BLK, 1), F32),
                        pltpu.VMEM((N_HEADS, BLK, LANE), F32),
                        pltpu.VMEM((N_HEADS, BLK, LANE), F32)],
        compiler_params=pltpu.CompilerParams(vmem_limit_bytes=VMEM_LIMIT),
        name="moba_attn",
    )(att, att, att, kmean_pad, bias)


def _diff_kernel(lambda_init, q_ref, k_ref, v_ref, bias_ref, dl_ref, g_ref, o_ref, m_ref, l_ref, acc_ref):
    i = pl.program_id(1)
    lane, halves = _head_masks(BF16)
    causal = _local_dist() >= 0
    _init_state(m_ref, l_ref, acc_ref)
    q = q_ref[0]
    qms = []
    for h in range(N_HEADS):
        slot, half = divmod(h, 2)
        qs = q[:, slot * LANE:(slot + 1) * LANE]
        for mp in range(2):
            lo = half * HEAD + mp * DIFF_DH
            qms.append(jnp.where((lane >= lo) & (lane < lo + DIFF_DH), qs, jnp.zeros((), BF16)))

    def step(j, c_of, mask):
        start = pl.multiple_of(j * BLK, BLK)
        k_j = k_ref[0, pl.ds(start, BLK), :]
        v_j = v_ref[0, pl.ds(start, BLK), :]
        for h in range(N_HEADS):
            slot = h // 2
            k_s = k_j[:, slot * LANE:(slot + 1) * LANE]
            v_s = v_j[:, slot * LANE:(slot + 1) * LANE]
            for mp in range(2):
                n = 2 * h + mp
                s = _nt(qms[n], k_s) + bias_ref[h]
                if mask is not None:
                    s = jnp.where(mask, s, NEG)
                _online_update(s, c_of(h), n, v_s, m_ref, l_ref, acc_ref)

    step(i, lambda h: 0.0, causal)

    def body(j, carry):
        step(j, lambda h: _block_offset(SLOPES_B[h], i, j), None)
        return carry

    lax.fori_loop(0, i, body, 0)

    dl = dl_ref[...]
    lam = (jnp.exp(jnp.sum(dl[0:1] * dl[1:2], axis=1, keepdims=True))
           - jnp.exp(jnp.sum(dl[2:3] * dl[3:4], axis=1, keepdims=True)) + lambda_init)
    outs = []
    for h in range(N_HEADS):
        half = h % 2
        o = acc_ref[2 * h] / l_ref[2 * h] - lam * (acc_ref[2 * h + 1] / l_ref[2 * h + 1])
        ms = jnp.sum(jnp.where(halves[half], o * o, 0.0), axis=1, keepdims=True) * (1.0 / HEAD)
        outs.append(o * lax.rsqrt(ms + NORM_EPS) * g_ref[...] * (1.0 - lambda_init))
    for slot in range(2):
        o_ref[0, :, slot * LANE:(slot + 1) * LANE] = jnp.where(
            halves[0], outs[2 * slot], outs[2 * slot + 1]).astype(o_ref.dtype)


def _diff(att, bias, diff_lambda, subln_g2, lambda_init, seq):
    b = att.shape[0]
    nq = seq // BLK
    return pl.pallas_call(
        functools.partial(_diff_kernel, lambda_init),
        out_shape=jax.ShapeDtypeStruct((b, seq, 256), BF16),
        grid=(b, nq),
        in_specs=[pl.BlockSpec((1, BLK, 256), lambda bi, i: (bi, i, 3)),
                  pl.BlockSpec((1, seq, 256), lambda bi, i: (bi, 0, 4)),
                  pl.BlockSpec((1, seq, 256), lambda bi, i: (bi, 0, 5)),
                  pl.BlockSpec((N_HEADS, BLK, BLK), lambda bi, i: (0, 0, 0)),
                  pl.BlockSpec((4, DIFF_DH), lambda bi, i: (0, 0)),
                  pl.BlockSpec((1, LANE), lambda bi, i: (0, 0))],
        out_specs=pl.BlockSpec((1, BLK, 256), lambda bi, i: (bi, i, 0)),
        scratch_shapes=[pltpu.VMEM((2 * N_HEADS, BLK, 1), F32),
                        pltpu.VMEM((2 * N_HEADS, BLK, 1), F32),
                        pltpu.VMEM((2 * N_HEADS, BLK, LANE), F32)],
        compiler_params=pltpu.CompilerParams(vmem_limit_bytes=VMEM_LIMIT),
        name="diff_attn",
    )(att, att, att, bias, diff_lambda, subln_g2)


def _dsa_kernel(topk, q_ref, k_ref, v_ref, iq_ref, ik_ref, iw_ref, bias_ref, tri_ref, o_ref,
                m_ref, l_ref, acc_ref, key_ref):
    i = pl.program_id(1)
    lane, halves = _head_masks(BF16)
    dist = _local_dist()
    _init_state(m_ref, l_ref, acc_ref)

    iq = iq_ref[0]
    iw = iw_ref[0]
    iqm = [jnp.where((lane >= h * IDX_DH) & (lane < (h + 1) * IDX_DH), iq, 0.0) for h in range(N_IDX_HEADS)]
    wcol = [jnp.sum(jnp.where(lane == h, iw, 0.0), axis=1, keepdims=True) * (N_IDX_HEADS ** -0.5)
            for h in range(N_IDX_HEADS)]

    def score_block(j, carry):
        start = pl.multiple_of(j * BLK, BLK)
        ik_j = ik_ref[0, pl.ds(start, BLK), :]
        sc = jnp.zeros((BLK, BLK), F32)
        for h in range(N_IDX_HEADS):
            rel = jnp.maximum(_nt(iqm[h], ik_j, HIGHEST) * (IDX_DH ** -0.5), 0.0)
            sc = sc + rel * wcol[h]
        limit = jnp.where(j < i, BLK, 0)
        sc = jnp.where(dist + limit >= 0, sc, NEG)
        bits = lax.bitcast_convert_type(sc, jnp.int32)
        key_ref[j] = jnp.where(bits < 0, INT_MIN - bits, bits)
        return carry

    lax.fori_loop(0, i + 1, score_block, 0)

    def count_where(pred):
        def blk(j, acc):
            hit = jnp.where(pred(key_ref[j]), 1.0, 0.0)
            return acc + hit[:, :LANE] + hit[:, LANE:]
        acc = lax.fori_loop(0, i + 1, blk, jnp.zeros((BLK, LANE), F32))
        return jnp.sum(acc, axis=1, keepdims=True)

    def search(step, ans):
        cand = ans + jnp.left_shift(jnp.int32(1), 31 - step)
        cnt = count_where(lambda key: key >= cand)
        return jnp.where(cnt >= topk, cand, ans)

    thr = lax.fori_loop(0, 32, search, jnp.full((BLK, 1), INT_MIN, jnp.int32))
    n_above = count_where(lambda key: key > thr)
    need = jnp.where(thr == KEY_NEG, 0.0, topk - n_above)

    q = q_ref[0]
    qms = []
    for h in range(N_HEADS):
        slot, half = divmod(h, 2)
        qms.append(jnp.where(halves[half], q[:, slot * LANE:(slot + 1) * LANE], jnp.zeros((), BF16)))

    def attend(j, taken):
        start = pl.multiple_of(j * BLK, BLK)
        k_j = k_ref[0, pl.ds(start, BLK), :]
        v_j = v_ref[0, pl.ds(start, BLK), :]
        key = key_ref[j]
        tie = jnp.where(key == thr, 1.0, 0.0)
        rank = jnp.dot(tie.astype(BF16), tri_ref[...], preferred_element_type=F32)
        tie_ok = jnp.where(rank <= need - taken, tie, 0.0)
        chosen = jnp.where(key > thr, 1.0, tie_ok) > 0.5
        for h in range(N_HEADS):
            slot = h // 2
            s = _nt(qms[h], k_j[:, slot * LANE:(slot + 1) * LANE]) + bias_ref[h]
            s = jnp.where(chosen, s, NEG)
            _online_update(s, _block_offset(SLOPES_D[h], i, j), h,
                           v_j[:, slot * LANE:(slot + 1) * LANE], m_ref, l_ref, acc_ref)
        return taken + jnp.sum(tie, axis=1, keepdims=True)

    lax.fori_loop(0, i + 1, attend, jnp.zeros((BLK, 1), F32))
    for slot in range(2):
        o0 = acc_ref[2 * slot] / l_ref[2 * slot]
        o1 = acc_ref[2 * slot + 1] / l_ref[2 * slot + 1]
        o_ref[0, :, slot * LANE:(slot + 1) * LANE] = jnp.where(halves[0], o0, o1).astype(o_ref.dtype)


def _dsa(att, idx, bias, tri, seq):
    b = att.shape[0]
    nq = seq // BLK
    topk = float(min(DSA_TOPK, seq // 4))
    return pl.pallas_call(
        functools.partial(_dsa_kernel, topk),
        out_shape=jax.ShapeDtypeStruct((b, seq, 256), BF16),
        grid=(b, nq),
        in_specs=[pl.BlockSpec((1, BLK, 256), lambda bi, i: (bi, i, 6)),
                  pl.BlockSpec((1, seq, 256), lambda bi, i: (bi, 0, 7)),
                  pl.BlockSpec((1, seq, 256), lambda bi, i: (bi, 0, 8)),
                  pl.BlockSpec((1, BLK, LANE), lambda bi, i: (bi, i, 0)),
                  pl.BlockSpec((1, seq, LANE), lambda bi, i: (bi, 0, 1)),
                  pl.BlockSpec((1, BLK, LANE), lambda bi, i: (bi, i, 2)),
                  pl.BlockSpec((N_HEADS, BLK, BLK), lambda bi, i: (0, 0, 0)),
                  pl.BlockSpec((BLK, BLK), lambda bi, i: (0, 0))],
        out_specs=pl.BlockSpec((1, BLK, 256), lambda bi, i: (bi, i, 0)),
        scratch_shapes=[pltpu.VMEM((N_HEADS, BLK, 1), F32),
                        pltpu.VMEM((N_HEADS, BLK, 1), F32),
                        pltpu.VMEM((N_HEADS, BLK, LANE), F32),
                        pltpu.VMEM((seq // BLK, BLK, BLK), jnp.int32)],
        compiler_params=pltpu.CompilerParams(vmem_limit_bytes=VMEM_LIMIT),
        name="dsa_attn",
    )(att, att, att, idx, idx, idx, bias, tri)


def _pool_kernel(tp, u_ref, halo_ref, w_ref, ps_ref, o_ref, a_ref, b_ref):
    i = pl.program_id(1)
    u = u_ref[0]
    body = POOL_HALO + tp
    a_ref[0:POOL_HALO] = jnp.zeros((POOL_HALO, 256), F32)
    b_ref[0:POOL_HALO] = jnp.zeros((POOL_HALO, 256), F32)
    a_ref[POOL_HALO:2 * POOL_HALO] = jnp.where(i > 0, halo_ref[0], 0.0)
    a_ref[2 * POOL_HALO:] = u
    group = jnp.right_shift(lax.broadcasted_iota(jnp.int32, (tp, 256), 1), 6)
    src, dst = a_ref, b_ref
    win = jnp.zeros((tp, 256), F32)
    shift = 1
    for g, w in enumerate(POOL_WINDOWS):
        while shift < w:
            dst[POOL_HALO:] = src[POOL_HALO:] + src[POOL_HALO - shift:POOL_HALO - shift + body]
            src, dst = dst, src
            shift *= 2
        win = jnp.where(group == g, src[2 * POOL_HALO:], win)
    t = i * tp + lax.broadcasted_iota(jnp.int32, (tp, 256), 0)
    width = jnp.where(group == 0, 2, jnp.where(group == 1, 4, jnp.where(group == 2, 8, 16)))
    cnt = jnp.minimum(t + 1, width).astype(F32)
    d = win / cnt - u
    y = jnp.dot(d.astype(BF16), w_ref[...], preferred_element_type=F32) * ps_ref[...]
    o_ref[0] = y.astype(o_ref.dtype)


def _pool(pu, w_pool_bd, pool_scale, tp):
    b, seq, _ = pu.shape
    per = tp // POOL_HALO
    return pl.pallas_call(
        functools.partial(_pool_kernel, tp),
        out_shape=jax.ShapeDtypeStruct((b, seq, 256), BF16),
        grid=(b, seq // tp),
        in_specs=[pl.BlockSpec((1, tp, 256), lambda bi, i: (bi, i, 0)),
                  pl.BlockSpec((1, POOL_HALO, 256), lambda bi, i: (bi, jnp.maximum(i * per - 1, 0), 0)),
                  pl.BlockSpec((256, 256), lambda bi, i: (0, 0)),
                  pl.BlockSpec((1, 256), lambda bi, i: (0, 0))],
        out_specs=pl.BlockSpec((1, tp, 256), lambda bi, i: (bi, i, 0)),
        scratch_shapes=[pltpu.VMEM((tp + 2 * POOL_HALO, 256), F32),
                        pltpu.VMEM((tp + 2 * POOL_HALO, 256), F32)],
        compiler_params=pltpu.CompilerParams(vmem_limit_bytes=VMEM_LIMIT),
        name="pool_mixer",
    )(pu, pu, w_pool_bd, pool_scale)


def _merge_kernel(x_ref, mod_ref, g_ref, ya_ref, yb_ref, yc_ref, yd_ref, wg_ref, bg_ref, wb_ref, wo_ref, o_ref):
    mod = mod_ref[0]
    x = x_ref[...]
    hb = _modulated_norm(x, g_ref[...], mod[0:1], mod[1:2]).astype(BF16)
    merged = jnp.zeros(x.shape, F32)
    for n, y_ref in enumerate((ya_ref, yb_ref, yc_ref, yd_ref)):
        gate = jax.nn.sigmoid(jnp.dot(hb, wg_ref[n], preferred_element_type=F32) + bg_ref[n])
        merged = merged + gate * jnp.dot(y_ref[...], wb_ref[n], preferred_element_type=F32)
    o_ref[...] = x + mod[2:3] * jnp.dot(merged.astype(BF16), wo_ref[...], preferred_element_type=F32)


def _resident(shape):
    zeros = (0,) * len(shape)
    return pl.BlockSpec(shape, lambda i: zeros, pipeline_mode=pl.Buffered(1))


def _merge(x2, mod, g, ys, w_gate, b_gate, w_branch, w_out, seq, tm):
    t, d = x2.shape
    tok = lambda w: pl.BlockSpec((tm, w), lambda i: (i, 0))
    return pl.pallas_call(
        _merge_kernel,
        out_shape=jax.ShapeDtypeStruct((t, d), F32),
        grid=(t // tm,),
        in_specs=[tok(d),
                  pl.BlockSpec((1, 6, d), lambda i: (i * tm // seq, 0, 0)),
                  _resident((1, d)),
                  tok(256), tok(256), tok(256), tok(256),
                  _resident(w_gate.shape), _resident(b_gate.shape), _resident(w_branch.shape),
                  _resident(w_out.shape)],
        out_specs=tok(d),
        compiler_params=pltpu.CompilerParams(vmem_limit_bytes=VMEM_LIMIT),
        name="gated_merge",
    )(x2, mod, g, *ys, w_gate, b_gate, w_branch, w_out)


def _ffn_kernel(final, x_ref, mod_ref, g_ref, w1_ref, w2_ref, fg_ref, o_ref):
    mod = mod_ref[0]
    x = x_ref[...]
    hb = _modulated_norm(x, g_ref[...], mod[3:4], mod[4:5]).astype(BF16)
    d = x.shape[1]
    ff = jnp.zeros(x.shape, F32)
    for c in range(w1_ref.shape[1] // d):
        a = jnp.maximum(jnp.dot(hb, w1_ref[:, c * d:(c + 1) * d], preferred_element_type=F32), 0.0)
        ff = ff + jnp.dot((a * a).astype(BF16), w2_ref[c * d:(c + 1) * d, :], preferred_element_type=F32)
    out = x + mod[5:6] * ff
    if final:
        ms = jnp.mean(out * out, axis=-1, keepdims=True)
        out = out * lax.rsqrt(ms + NORM_EPS) * fg_ref[...]
    o_ref[...] = out


def _ffn(x2, mod, g, w1, w2, final_g, final, seq, tm):
    t, d = x2.shape
    tok = pl.BlockSpec((tm, d), lambda i: (i, 0))
    return pl.pallas_call(
        functools.partial(_ffn_kernel, final),
        out_shape=jax.ShapeDtypeStruct((t, d), F32),
        grid=(t // tm,),
        in_specs=[tok,
                  pl.BlockSpec((1, 6, d), lambda i: (i * tm // seq, 0, 0)),
                  _resident((1, d)), _resident(w1.shape), _resident(w2.shape), _resident((1, d))],
        out_specs=tok,
        compiler_params=pltpu.CompilerParams(vmem_limit_bytes=VMEM_LIMIT),
        name="ffn",
    )(x2, mod, g, w1, w2, final_g)


def _alibi_tables(slopes):
    d = (np.arange(BLK)[:, None] - np.arange(BLK)[None, :]).astype(np.float32)
    return jnp.asarray(np.stack([-np.float32(m) * d for m in slopes]).astype(np.float32))


def _reorder_w_in(w_in):
    o = np.cumsum([0, 256, 256, 256, 256, 256, 256, 256, 256, 256, 256, 128, 32, 4])
    piece = lambda n: w_in[:, o[n]:o[n + 1]]
    d = w_in.shape[0]
    cols = [piece(n) for n in (0, 1, 2, 3, 4, 5, 7, 8, 9, 6, 10)]
    cols += [piece(11)] * (LANE // IDX_DH)
    cols += [piece(12), jnp.zeros((d, LANE - N_IDX_HEADS), w_in.dtype)]
    return jnp.concatenate(cols, axis=1)


def _q_colscale():
    s = np.ones((1, ATT_W), np.float32)
    s[0, 0:256] = HEAD ** -0.5
    s[0, 768:1024] = DIFF_DH ** -0.5
    s[0, 1536:1792] = HEAD ** -0.5
    return jnp.asarray(s)


def _block_diag(w_pool):
    g, c, _ = w_pool.shape
    out = jnp.zeros((g * c, g * c), w_pool.dtype)
    for n in range(g):
        out = out.at[n * c:(n + 1) * c, n * c:(n + 1) * c].set(w_pool[n])
    return out


def kernel(x, c, w_ada, b_ada, norm1_g, w_in, diff_lambda, diff_subln_g, w_pool, pool_scale, w_gate, b_gate,
           w_branch, w_out, norm2_g, w_ff1, w_ff2, final_norm_g):
    b, seq, d = x.shape
    depth = w_ada.shape[0]
    t = b * seq
    tm = 512
    assert seq % tm == 0 and seq // BLK <= LANE
    mods = _modulation(c, w_ada, b_ada).reshape(depth, b, 6, d)
    bias_a, bias_b, bias_d = _alibi_tables(SLOPES_A), _alibi_tables(SLOPES_B), _alibi_tables(SLOPES_D)
    tri = jnp.asarray(np.triu(np.ones((BLK, BLK), np.float32))).astype(BF16)
    colscale = _q_colscale()
    x2 = x.reshape(t, d)
    for l in range(depth):
        mod = mods[l]
        g1 = norm1_g[l].reshape(1, d)
        att, pu, idx, kmean = _in_proj(x2, mod, g1, _reorder_w_in(w_in[l]).astype(BF16), colscale, seq, tm)
        att = att.reshape(b, seq, ATT_W)
        kmean_pad = jnp.pad(kmean.reshape(b, seq // BLK, 256), ((0, 0), (0, LANE - seq // BLK), (0, 0)))
        y_a = _moba(att, kmean_pad, bias_a, seq)
        lambda_init = 0.8 - 0.6 * math.exp(-0.3 * l)
        y_b = _diff(att, bias_b, diff_lambda[l], jnp.tile(diff_subln_g[l], 2).reshape(1, LANE), lambda_init, seq)
        y_c = _pool(pu.reshape(b, seq, 256), _block_diag(w_pool[l]).astype(BF16), pool_scale[l].reshape(1, 256), tm)
        y_d = _dsa(att, idx.reshape(b, seq, 3 * LANE), bias_d, tri, seq)
        ys = [y.reshape(t, 256) for y in (y_a, y_b, y_c, y_d)]
        x2 = _merge(x2, mod, g1, ys, w_gate[l].astype(BF16), b_gate[l][:, None, :], w_branch[l].astype(BF16),
                    w_out[l].astype(BF16), seq, tm)
        x2 = _ffn(x2, mod, norm2_g[l].reshape(1, d), w_ff1[l].astype(BF16), w_ff2[l].astype(BF16),
                  final_norm_g.reshape(1, d), l == depth - 1, seq, tm)
    return x2.reshape(b, seq, d)
```

```python
import functools
import math

import numpy as np
import jax
import jax.numpy as jnp
from jax import lax
from jax.experimental import pallas as pl
from jax.experimental.pallas import tpu as pltpu

F32 = jnp.float32
BF16 = jnp.bfloat16
HIGHEST = lax.Precision.HIGHEST

NEG = -1e30
BELOW_NEG = -3.0e38
NORM_EPS = 1e-6
LOG2E = 1.4426950408889634
BLK = 256
LANE = 128
SUBLANE = 8
HEAD = 64
N_HEADS = 4
DIFF_DH = 32
MOBA_TOPK = 3
DSA_TOPK = 256
IDX_DH = 32
N_IDX_HEADS = 4
POOL_WINDOWS = (2, 4, 8, 16)
POOL_HALO = 16
INT_MIN = -(2 ** 31)
VMEM_LIMIT = 56 * 1024 * 1024

_ALL_SLOPES = [float(np.float32(2.0 ** (-8.0 * i / 12))) for i in range(1, 13)]
SLOPES_A = _ALL_SLOPES[0::3]
SLOPES_B = _ALL_SLOPES[1::3]
SLOPES_D = _ALL_SLOPES[2::3]

ATT_W = 6 * 256
V_OFF = ATT_W
V_W = 3 * 256
PU_OFF = V_OFF + V_W
IDX_OFF = PU_OFF + 256
IN_W = IDX_OFF + 3 * LANE


def _key_of_neg():
    bits = int(np.array(NEG, np.float32).view(np.int32))
    return INT_MIN - bits


KEY_NEG = _key_of_neg()


def _nt(a, b, precision=None):
    return lax.dot_general(a, b, (((1,), (1,)), ((), ())), preferred_element_type=F32, precision=precision)


def _modulated_norm(x, g, shift, scale):
    ms = jnp.mean(x * x, axis=-1, keepdims=True)
    y = x * lax.rsqrt(ms + NORM_EPS) * g
    return y * (1.0 + scale) + shift


def _mod_kernel(c_ref, w_ref, b_ref, o_ref):
    c = c_ref[...]
    act = c * jax.nn.sigmoid(c)
    o_ref[0] = jnp.dot(act, w_ref[0], preferred_element_type=F32, precision=HIGHEST) + b_ref[0]


def _modulation(c, w_ada, b_ada):
    depth, d, n = w_ada.shape
    b = c.shape[0]
    tn = n // 4
    return pl.pallas_call(
        _mod_kernel,
        out_shape=jax.ShapeDtypeStruct((depth, b, n), F32),
        grid=(depth, n // tn),
        in_specs=[pl.BlockSpec((b, d), lambda l, j: (0, 0)),
                  pl.BlockSpec((1, d, tn), lambda l, j: (l, 0, j)),
                  pl.BlockSpec((1, 1, tn), lambda l, j: (l, 0, j))],
        out_specs=pl.BlockSpec((1, b, tn), lambda l, j: (l, 0, j)),
        compiler_params=pltpu.CompilerParams(vmem_limit_bytes=VMEM_LIMIT),
        name="adaln_mod",
    )(c, w_ada, b_ada.reshape(depth, 1, n))


def _in_kernel(x_ref, mod_ref, g_ref, w_ref, cs_ref, att_ref, vt_ref, pu_ref, idx_ref, iwt_ref, km_ref):
    mod = mod_ref[0]
    h = _modulated_norm(x_ref[...], g_ref[...], mod[0:1], mod[1:2])
    proj = jnp.dot(h.astype(BF16), w_ref[...], preferred_element_type=F32)
    att_ref[...] = (proj[:, :ATT_W] * cs_ref[...]).astype(BF16)
    pu_ref[...] = proj[:, PU_OFF:PU_OFF + 256]
    idx_ref[...] = proj[:, IDX_OFF:IDX_OFF + 2 * LANE].astype(BF16)
    for r in range(km_ref.shape[0]):
        rows = slice(r * BLK, (r + 1) * BLK)
        km_ref[r] = jnp.mean(proj[rows, 256:512], axis=0, keepdims=True)
        for g in range(3):
            vt_ref[0, r, g * 256:(g + 1) * 256, :] = proj[rows, V_OFF + g * 256:V_OFF + (g + 1) * 256].T.astype(BF16)
        iwt_ref[0, r] = proj[rows, IDX_OFF + 2 * LANE:].T[0:SUBLANE, :]


def _in_proj(x2, mod, g, w_in_r, colscale, b, seq, tm):
    t, d = x2.shape
    nb = tm // BLK
    per_seq = seq // tm
    return pl.pallas_call(
        _in_kernel,
        out_shape=(jax.ShapeDtypeStruct((t, ATT_W), BF16),
                   jax.ShapeDtypeStruct((b, seq // BLK, V_W, BLK), BF16),
                   jax.ShapeDtypeStruct((t, 256), F32),
                   jax.ShapeDtypeStruct((t, 2 * LANE), BF16),
                   jax.ShapeDtypeStruct((b, seq // BLK, SUBLANE, BLK), F32),
                   jax.ShapeDtypeStruct((t // BLK, 1, 256), F32)),
        grid=(t // tm,),
        in_specs=[pl.BlockSpec((tm, d), lambda i: (i, 0)),
                  pl.BlockSpec((1, 6, d), lambda i: (i // per_seq, 0, 0)),
                  pl.BlockSpec((1, d), lambda i: (0, 0)),
                  pl.BlockSpec((d, IN_W), lambda i: (0, 0)),
                  pl.BlockSpec((1, ATT_W), lambda i: (0, 0))],
        out_specs=(pl.BlockSpec((tm, ATT_W), lambda i: (i, 0)),
                   pl.BlockSpec((1, nb, V_W, BLK), lambda i: (i // per_seq, i % per_seq, 0, 0)),
                   pl.BlockSpec((tm, 256), lambda i: (i, 0)),
                   pl.BlockSpec((tm, 2 * LANE), lambda i: (i, 0)),
                   pl.BlockSpec((1, nb, SUBLANE, BLK), lambda i: (i // per_seq, i % per_seq, 0, 0)),
                   pl.BlockSpec((nb, 1, 256), lambda i: (i, 0, 0))),
        compiler_params=pltpu.CompilerParams(vmem_limit_bytes=VMEM_LIMIT),
        name="in_proj",
    )(x2, mod, g, w_in_r, colscale)


def _lane_halves():
    lane = lax.broadcasted_iota(jnp.int32, (BLK, LANE), 1)
    return lane, [lane < HEAD, lane >= HEAD]


def _causal_t():
    sl = lax.broadcasted_iota(jnp.int32, (BLK, BLK), 0)
    tl = lax.broadcasted_iota(jnp.int32, (BLK, BLK), 1)
    return tl - sl


def _init_state(m_ref, l_ref, acc_ref):
    m_ref[...] = jnp.full(m_ref.shape, NEG, F32)
    l_ref[...] = jnp.zeros(l_ref.shape, F32)
    acc_ref[...] = jnp.zeros(acc_ref.shape, F32)


def _flash_step(s, c, n, vt_h, m_ref, l_ref, acc_ref):
    m_old = m_ref[n]
    m_new = jnp.maximum(m_old, jnp.max(s, axis=0, keepdims=True) + c)
    p = jnp.exp2(s - (m_new - c))
    alpha = jnp.exp2(m_old - m_new)
    l_ref[n] = alpha * l_ref[n] + jnp.sum(p, axis=0, keepdims=True)
    acc_ref[n] = alpha * acc_ref[n] + jnp.dot(vt_h, p.astype(BF16), preferred_element_type=F32)
    m_ref[n] = m_new


def _block_offset(slope, i, j):
    steps = jnp.full((1, 1), i - j, jnp.int32).astype(F32)
    return steps * (-slope * LOG2E * BLK)


def _write_heads(o_ref, outs):
    for slot in range(2):
        pair = jnp.concatenate([outs[2 * slot], outs[2 * slot + 1]], axis=0)
        o_ref[0, :, slot * LANE:(slot + 1) * LANE] = pair.T.astype(o_ref.dtype)


def _attn_specs(seq, q_col, k_col, v_group):
    nblk = seq // BLK
    return [pl.BlockSpec((1, BLK, 256), lambda bi, i: (bi, i, q_col)),
            pl.BlockSpec((1, seq, 256), lambda bi, i: (bi, 0, k_col)),
            pl.BlockSpec((1, nblk, 256, BLK), lambda bi, i: (bi, 0, v_group, 0))]


def _state_scratch(n):
    return [pltpu.VMEM((n, 1, BLK), F32), pltpu.VMEM((n, 1, BLK), F32), pltpu.VMEM((n, HEAD, BLK), F32)]


def _moba_kernel(q_ref, k_ref, vt_ref, km_ref, bias_ref, o_ref, m_ref, l_ref, acc_ref, sel_ref):
    i = pl.program_id(1)
    nb = km_ref.shape[1]
    _, halves = _lane_halves()
    row = lax.broadcasted_iota(jnp.int32, (nb, BLK), 0)
    row_f = row.astype(F32)
    causal = _causal_t() >= 0
    _init_state(m_ref, l_ref, acc_ref)
    q = q_ref[0]
    own = pl.multiple_of(i * BLK, BLK)
    k_own = k_ref[0, pl.ds(own, BLK), :]
    vt_own = vt_ref[0, i]
    km = km_ref[0]
    qms = []
    for h in range(N_HEADS):
        slot, half = divmod(h, 2)
        cols = slice(slot * LANE, (slot + 1) * LANE)
        qm = jnp.where(halves[half], q[:, cols], jnp.zeros((), BF16))
        qms.append(qm)
        g = jnp.where(row < i, _nt(km[:, cols], qm.astype(F32), HIGHEST), NEG)
        sel = jnp.zeros((nb, BLK), F32)
        for _ in range(MOBA_TOPK):
            mx = jnp.max(g, axis=0, keepdims=True)
            first = jnp.min(jnp.where(g == mx, row_f, 1e9), axis=0, keepdims=True)
            pick = row_f == first
            sel = jnp.where(pick, 1.0, sel)
            g = jnp.where(pick, BELOW_NEG, g)
        sel_ref[h] = jnp.where(row < i, sel, 0.0)
        s = jnp.where(causal, _nt(k_own[:, cols], qm) + bias_ref[h], NEG)
        _flash_step(s, 0.0, h, vt_own[h * HEAD:(h + 1) * HEAD, :], m_ref, l_ref, acc_ref)

    def body(j, carry):
        start = pl.multiple_of(j * BLK, BLK)
        k_j = k_ref[0, pl.ds(start, BLK), :]
        vt_j = vt_ref[0, j]
        for h in range(N_HEADS):
            slot = h // 2
            chosen = sel_ref[h, pl.ds(j, 1), :] > 0.5
            s = _nt(k_j[:, slot * LANE:(slot + 1) * LANE], qms[h]) + bias_ref[h]
            s = jnp.where(chosen, s, NEG)
            _flash_step(s, _block_offset(SLOPES_A[h], i, j), h, vt_j[h * HEAD:(h + 1) * HEAD, :],
                        m_ref, l_ref, acc_ref)
        return carry

    lax.fori_loop(0, i, body, 0)
    _write_heads(o_ref, [acc_ref[h] / l_ref[h] for h in range(N_HEADS)])


def _moba(att, vt, kmean_pad, bias, seq):
    b = att.shape[0]
    nb = kmean_pad.shape[1]
    return pl.pallas_call(
        _moba_kernel,
        out_shape=jax.ShapeDtypeStruct((b, seq, 256), BF16),
        grid=(b, seq // BLK),
        in_specs=_attn_specs(seq, 0, 1, 0) + [
            pl.BlockSpec((1, nb, 256), lambda bi, i: (bi, 0, 0)),
            pl.BlockSpec((N_HEADS, BLK, BLK), lambda bi, i: (0, 0, 0))],
        out_specs=pl.BlockSpec((1, BLK, 256), lambda bi, i: (bi, i, 0)),
        scratch_shapes=_state_scratch(N_HEADS) + [pltpu.VMEM((N_HEADS, nb, BLK), F32)],
        compiler_params=pltpu.CompilerParams(vmem_limit_bytes=VMEM_LIMIT),
        name="moba_attn",
    )(att, att, vt, kmean_pad, bias)


def _diff_kernel(lambda_init, q_ref, k_ref, vt_ref, bias_ref, dl_ref, g_ref, o_ref, m_ref, l_ref, acc_ref):
    i = pl.program_id(1)
    lane, halves = _lane_halves()
    causal = _causal_t() >= 0
    _init_state(m_ref, l_ref, acc_ref)
    q = q_ref[0]
    qms = []
    for h in range(N_HEADS):
        slot, half = divmod(h, 2)
        qs = q[:, slot * LANE:(slot + 1) * LANE]
        for mp in range(2):
            lo = half * HEAD + mp * DIFF_DH
            qms.append(jnp.where((lane >= lo) & (lane < lo + DIFF_DH), qs, jnp.zeros((), BF16)))

    def step(j, c_of, mask):
        start = pl.multiple_of(j * BLK, BLK)
        k_j = k_ref[0, pl.ds(start, BLK), :]
        vt_j = vt_ref[0, j]
        for h in range(N_HEADS):
            slot = h // 2
            k_s = k_j[:, slot * LANE:(slot + 1) * LANE]
            vt_h = vt_j[h * HEAD:(h + 1) * HEAD, :]
            for mp in range(2):
                n = 2 * h + mp
                s = _nt(k_s, qms[n]) + bias_ref[h]
                if mask is not None:
                    s = jnp.where(mask, s, NEG)
                _flash_step(s, c_of(h), n, vt_h, m_ref, l_ref, acc_ref)

    step(i, lambda h: 0.0, causal)

    def body(j, carry):
        step(j, lambda h: _block_offset(SLOPES_B[h], i, j), None)
        return carry

    lax.fori_loop(0, i, body, 0)

    dl = dl_ref[...]
    lam = (jnp.exp(jnp.sum(dl[0:1] * dl[1:2], axis=1, keepdims=True))
           - jnp.exp(jnp.sum(dl[2:3] * dl[3:4], axis=1, keepdims=True)) + lambda_init)
    for slot in range(2):
        pair = []
        for h in (2 * slot, 2 * slot + 1):
            pair.append(acc_ref[2 * h] / l_ref[2 * h] - lam * (acc_ref[2 * h + 1] / l_ref[2 * h + 1]))
        o = jnp.concatenate(pair, axis=0).T
        sq = o * o
        ms = jnp.where(halves[0],
                       jnp.sum(jnp.where(halves[0], sq, 0.0), axis=1, keepdims=True),
                       jnp.sum(jnp.where(halves[1], sq, 0.0), axis=1, keepdims=True)) * (1.0 / HEAD)
        y = o * lax.rsqrt(ms + NORM_EPS) * g_ref[...] * (1.0 - lambda_init)
        o_ref[0, :, slot * LANE:(slot + 1) * LANE] = y.astype(o_ref.dtype)


def _diff(att, vt, bias, diff_lambda, subln_g2, lambda_init, seq):
    b = att.shape[0]
    return pl.pallas_call(
        functools.partial(_diff_kernel, lambda_init),
        out_shape=jax.ShapeDtypeStruct((b, seq, 256), BF16),
        grid=(b, seq // BLK),
        in_specs=_attn_specs(seq, 2, 3, 1) + [
            pl.BlockSpec((N_HEADS, BLK, BLK), lambda bi, i: (0, 0, 0)),
            pl.BlockSpec((4, DIFF_DH), lambda bi, i: (0, 0)),
            pl.BlockSpec((1, LANE), lambda bi, i: (0, 0))],
        out_specs=pl.BlockSpec((1, BLK, 256), lambda bi, i: (bi, i, 0)),
        scratch_shapes=_state_scratch(2 * N_HEADS),
        compiler_params=pltpu.CompilerParams(vmem_limit_bytes=VMEM_LIMIT),
        name="diff_attn",
    )(att, att, vt, bias, diff_lambda, subln_g2)


def _dsa_kernel(topk, q_ref, k_ref, vt_ref, iq_ref, ik_ref, iwt_ref, bias_ref, tri_ref, o_ref,
                m_ref, l_ref, acc_ref, key_ref):
    i = pl.program_id(1)
    lane, halves = _lane_halves()
    dist = _causal_t()
    _init_state(m_ref, l_ref, acc_ref)

    iq = iq_ref[0]
    iqm = [jnp.where((lane >= h * IDX_DH) & (lane < (h + 1) * IDX_DH), iq, jnp.zeros((), BF16))
           for h in range(N_IDX_HEADS)]
    iwt = iwt_ref[0, 0]
    wrow = [iwt[h:h + 1, :] * (N_IDX_HEADS ** -0.5) for h in range(N_IDX_HEADS)]

    def score_block(j, carry):
        start = pl.multiple_of(j * BLK, BLK)
        ik_j = ik_ref[0, pl.ds(start, BLK), :]
        sc = jnp.zeros((BLK, BLK), F32)
        for h in range(N_IDX_HEADS):
            rel = jnp.maximum(_nt(ik_j, iqm[h]) * (IDX_DH ** -0.5), 0.0)
            sc = sc + rel * wrow[h]
        limit = jnp.where(j < i, BLK, 0)
        sc = jnp.where(dist + limit >= 0, sc, NEG)
        bits = lax.bitcast_convert_type(sc, jnp.int32)
        key_ref[j] = jnp.where(bits < 0, INT_MIN - bits, bits)
        return carry

    lax.fori_loop(0, i + 1, score_block, 0)

    def count_where(pred):
        def blk(j, acc):
            hit = jnp.where(pred(key_ref[j]), 1.0, 0.0)
            return acc + jnp.sum(hit.reshape(BLK // SUBLANE, SUBLANE, BLK), axis=0)
        acc = lax.fori_loop(0, i + 1, blk, jnp.zeros((SUBLANE, BLK), F32))
        return jnp.sum(acc, axis=0, keepdims=True)

    def search(step, ans):
        cand = ans + jnp.left_shift(jnp.int32(1), 31 - step)
        cnt = count_where(lambda key: key >= cand)
        return jnp.where(cnt >= topk, cand, ans)

    thr = lax.fori_loop(0, 32, search, jnp.full((1, BLK), INT_MIN, jnp.int32))
    n_above = count_where(lambda key: key > thr)
    need = jnp.where(thr == KEY_NEG, 0.0, topk - n_above)

    q = q_ref[0]
    qms = []
    for h in range(N_HEADS):
        slot, half = divmod(h, 2)
        qms.append(jnp.where(halves[half], q[:, slot * LANE:(slot + 1) * LANE], jnp.zeros((), BF16)))

    def attend(j, taken):
        start = pl.multiple_of(j * BLK, BLK)
        k_j = k_ref[0, pl.ds(start, BLK), :]
        vt_j = vt_ref[0, j]
        key = key_ref[j]
        tie = jnp.where(key == thr, 1.0, 0.0)
        rank = jnp.dot(tri_ref[...], tie.astype(BF16), preferred_element_type=F32)
        tie_ok = jnp.where(rank <= need - taken, tie, 0.0)
        chosen = jnp.where(key > thr, 1.0, tie_ok) > 0.5
        for h in range(N_HEADS):
            slot = h // 2
            s = _nt(k_j[:, slot * LANE:(slot + 1) * LANE], qms[h]) + bias_ref[h]
            s = jnp.where(chosen, s, NEG)
            _flash_step(s, _block_offset(SLOPES_D[h], i, j), h, vt_j[h * HEAD:(h + 1) * HEAD, :],
                        m_ref, l_ref, acc_ref)
        return taken + jnp.sum(tie, axis=0, keepdims=True)

    lax.fori_loop(0, i + 1, attend, jnp.zeros((1, BLK), F32))
    _write_heads(o_ref, [acc_ref[h] / l_ref[h] for h in range(N_HEADS)])


def _dsa(att, vt, idx, iwt, bias, tri, seq):
    b = att.shape[0]
    nblk = seq // BLK
    topk = float(min(DSA_TOPK, seq // 4))
    return pl.pallas_call(
        functools.partial(_dsa_kernel, topk),
        out_shape=jax.ShapeDtypeStruct((b, seq, 256), BF16),
        grid=(b, nblk),
        in_specs=_attn_specs(seq, 4, 5, 2) + [
            pl.BlockSpec((1, BLK, LANE), lambda bi, i: (bi, i, 0)),
            pl.BlockSpec((1, seq, LANE), lambda bi, i: (bi, 0, 1)),
            pl.BlockSpec((1, 1, SUBLANE, BLK), lambda bi, i: (bi, i, 0, 0)),
            pl.BlockSpec((N_HEADS, BLK, BLK), lambda bi, i: (0, 0, 0)),
            pl.BlockSpec((BLK, BLK), lambda bi, i: (0, 0))],
        out_specs=pl.BlockSpec((1, BLK, 256), lambda bi, i: (bi, i, 0)),
        scratch_shapes=_state_scratch(N_HEADS) + [pltpu.VMEM((nblk, BLK, BLK), jnp.int32)],
        compiler_params=pltpu.CompilerParams(vmem_limit_bytes=VMEM_LIMIT),
        name="dsa_attn",
    )(att, att, vt, idx, idx, iwt, bias, tri)


def _pool_kernel(tp, u_ref, halo_ref, w_ref, ps_ref, o_ref, a_ref, b_ref):
    i = pl.program_id(1)
    u = u_ref[0]
    body = POOL_HALO + tp
    a_ref[0:POOL_HALO] = jnp.zeros((POOL_HALO, 256), F32)
    b_ref[0:POOL_HALO] = jnp.zeros((POOL_HALO, 256), F32)
    a_ref[POOL_HALO:2 * POOL_HALO] = jnp.where(i > 0, halo_ref[0], 0.0)
    a_ref[2 * POOL_HALO:] = u
    group = jnp.right_shift(lax.broadcasted_iota(jnp.int32, (tp, 256), 1), 6)
    src, dst = a_ref, b_ref
    win = jnp.zeros((tp, 256), F32)
    shift = 1
    for g, w in enumerate(POOL_WINDOWS):
        while shift < w:
            dst[POOL_HALO:] = src[POOL_HALO:] + src[POOL_HALO - shift:POOL_HALO - shift + body]
            src, dst = dst, src
            shift *= 2
        win = jnp.where(group == g, src[2 * POOL_HALO:], win)
    t = i * tp + lax.broadcasted_iota(jnp.int32, (tp, 256), 0)
    width = jnp.where(group == 0, 2, jnp.where(group == 1, 4, jnp.where(group == 2, 8, 16)))
    cnt = jnp.minimum(t + 1, width).astype(F32)
    d = win / cnt - u
    y = jnp.dot(d.astype(BF16), w_ref[...], preferred_element_type=F32) * ps_ref[...]
    o_ref[0] = y.astype(o_ref.dtype)


def _pool(pu, w_pool_bd, pool_scale, tp):
    b, seq, _ = pu.shape
    per = tp // POOL_HALO
    return pl.pallas_call(
        functools.partial(_pool_kernel, tp),
        out_shape=jax.ShapeDtypeStruct((b, seq, 256), BF16),
        grid=(b, seq // tp),
        in_specs=[pl.BlockSpec((1, tp, 256), lambda bi, i: (bi, i, 0)),
                  pl.BlockSpec((1, POOL_HALO, 256), lambda bi, i: (bi, jnp.maximum(i * per - 1, 0), 0)),
                  pl.BlockSpec((256, 256), lambda bi, i: (0, 0)),
                  pl.BlockSpec((1, 256), lambda bi, i: (0, 0))],
        out_specs=pl.BlockSpec((1, tp, 256), lambda bi, i: (bi, i, 0)),
        scratch_shapes=[pltpu.VMEM((tp + 2 * POOL_HALO, 256), F32),
                        pltpu.VMEM((tp + 2 * POOL_HALO, 256), F32)],
        compiler_params=pltpu.CompilerParams(vmem_limit_bytes=VMEM_LIMIT),
        name="pool_mixer",
    )(pu, pu, w_pool_bd, pool_scale)


def _merge_kernel(x_ref, mod_ref, g_ref, ya_ref, yb_ref, yc_ref, yd_ref, wg_ref, bg_ref, wb_ref, wo_ref, o_ref):
    mod = mod_ref[0]
    x = x_ref[...]
    hb = _modulated_norm(x, g_ref[...], mod[0:1], mod[1:2]).astype(BF16)
    merged = jnp.zeros(x.shape, F32)
    for n, y_ref in enumerate((ya_ref, yb_ref, yc_ref, yd_ref)):
        gate = jax.nn.sigmoid(jnp.dot(hb, wg_ref[n], preferred_element_type=F32) + bg_ref[n])
        merged = merged + gate * jnp.dot(y_ref[...], wb_ref[n], preferred_element_type=F32)
    o_ref[...] = x + mod[2:3] * jnp.dot(merged.astype(BF16), wo_ref[...], preferred_element_type=F32)


def _resident(shape):
    zeros = (0,) * len(shape)
    return pl.BlockSpec(shape, lambda i: zeros, pipeline_mode=pl.Buffered(1))


def _merge(x2, mod, g, ys, w_gate, b_gate, w_branch, w_out, seq, tm):
    t, d = x2.shape
    tok = lambda w: pl.BlockSpec((tm, w), lambda i: (i, 0))
    return pl.pallas_call(
        _merge_kernel,
        out_shape=jax.ShapeDtypeStruct((t, d), F32),
        grid=(t // tm,),
        in_specs=[tok(d),
                  pl.BlockSpec((1, 6, d), lambda i: (i * tm // seq, 0, 0)),
                  _resident((1, d)),
                  tok(256), tok(256), tok(256), tok(256),
                  _resident(w_gate.shape), _resident(b_gate.shape), _resident(w_branch.shape),
                  _resident(w_out.shape)],
        out_specs=tok(d),
        compiler_params=pltpu.CompilerParams(vmem_limit_bytes=VMEM_LIMIT),
        name="gated_merge",
    )(x2, mod, g, *ys, w_gate, b_gate, w_branch, w_out)


def _ffn_kernel(final, x_ref, mod_ref, g_ref, w1_ref, w2_ref, fg_ref, o_ref):
    mod = mod_ref[0]
    x = x_ref[...]
    hb = _modulated_norm(x, g_ref[...], mod[3:4], mod[4:5]).astype(BF16)
    d = x.shape[1]
    ff = jnp.zeros(x.shape, F32)
    for c in range(w1_ref.shape[1] // d):
        a = jnp.maximum(jnp.dot(hb, w1_ref[:, c * d:(c + 1) * d], preferred_element_type=F32), 0.0)
        ff = ff + jnp.dot((a * a).astype(BF16), w2_ref[c * d:(c + 1) * d, :], preferred_element_type=F32)
    out = x + mod[5:6] * ff
    if final:
        ms = jnp.mean(out * out, axis=-1, keepdims=True)
        out = out * lax.rsqrt(ms + NORM_EPS) * fg_ref[...]
    o_ref[...] = out


def _ffn(x2, mod, g, w1, w2, final_g, final, seq, tm):
    t, d = x2.shape
    tok = pl.BlockSpec((tm, d), lambda i: (i, 0))
    return pl.pallas_call(
        functools.partial(_ffn_kernel, final),
        out_shape=jax.ShapeDtypeStruct((t, d), F32),
        grid=(t // tm,),
        in_specs=[tok,
                  pl.BlockSpec((1, 6, d), lambda i: (i * tm // seq, 0, 0)),
                  _resident((1, d)), _resident(w1.shape), _resident(w2.shape), _resident((1, d))],
        out_specs=tok,
        compiler_params=pltpu.CompilerParams(vmem_limit_bytes=VMEM_LIMIT),
        name="ffn",
    )(x2, mod, g, w1, w2, final_g)


def _alibi_tables(slopes):
    s_local = np.broadcast_to(np.arange(BLK, dtype=np.float64)[:, None], (BLK, BLK))
    return jnp.asarray(np.stack([m * LOG2E * s_local for m in slopes]).astype(np.float32))


def _reorder_w_in(w_in):
    o = np.cumsum([0, 256, 256, 256, 256, 256, 256, 256, 256, 256, 256, 128, 32, 4])
    piece = lambda n: w_in[:, o[n]:o[n + 1]]
    d = w_in.shape[0]
    cols = [piece(n) for n in (0, 1, 3, 4, 7, 8, 2, 5, 9, 6, 10)]
    cols += [piece(11)] * (LANE // IDX_DH)
    cols += [piece(12), jnp.zeros((d, LANE - N_IDX_HEADS), w_in.dtype)]
    return jnp.concatenate(cols, axis=1)


def _q_colscale():
    s = np.ones((1, ATT_W), np.float32)
    s[0, 0:256] = HEAD ** -0.5 * LOG2E
    s[0, 512:768] = DIFF_DH ** -0.5 * LOG2E
    s[0, 1024:1280] = HEAD ** -0.5 * LOG2E
    return jnp.asarray(s)


def _block_diag(w_pool):
    g, c, _ = w_pool.shape
    out = jnp.zeros((g * c, g * c), w_pool.dtype)
    for n in range(g):
        out = out.at[n * c:(n + 1) * c, n * c:(n + 1) * c].set(w_pool[n])
    return out


def kernel(x, c, w_ada, b_ada, norm1_g, w_in, diff_lambda, diff_subln_g, w_pool, pool_scale, w_gate, b_gate,
           w_branch, w_out, norm2_g, w_ff1, w_ff2, final_norm_g):
    b, seq, d = x.shape
    depth = w_ada.shape[0]
    t = b * seq
    tm = 512
    nblk = seq // BLK
    nb_pad = -(-nblk // SUBLANE) * SUBLANE
    assert seq % tm == 0
    mods = _modulation(c, w_ada, b_ada).reshape(depth, b, 6, d)
    bias_a, bias_b, bias_d = _alibi_tables(SLOPES_A), _alibi_tables(SLOPES_B), _alibi_tables(SLOPES_D)
    tri = jnp.asarray(np.tril(np.ones((BLK, BLK), np.float32))).astype(BF16)
    colscale = _q_colscale()
    x2 = x.reshape(t, d)
    for l in range(depth):
        mod = mods[l]
        g1 = norm1_g[l].reshape(1, d)
        att, vt, pu, idx, iwt, kmean = _in_proj(x2, mod, g1, _reorder_w_in(w_in[l]).astype(BF16), colscale,
                                                b, seq, tm)
        att = att.reshape(b, seq, ATT_W)
        kmean_pad = jnp.pad(kmean.reshape(b, nblk, 256), ((0, 0), (0, nb_pad - nblk), (0, 0)))
        y_a = _moba(att, vt, kmean_pad, bias_a, seq)
        lambda_init = 0.8 - 0.6 * math.exp(-0.3 * l)
        y_b = _diff(att, vt, bias_b, diff_lambda[l], jnp.tile(diff_subln_g[l], 2).reshape(1, LANE), lambda_init, seq)
        y_c = _pool(pu.reshape(b, seq, 256), _block_diag(w_pool[l]).astype(BF16), pool_scale[l].reshape(1, 256), tm)
        y_d = _dsa(att, vt, idx.reshape(b, seq, 2 * LANE), iwt, bias_d, tri, seq)
        ys = [y.reshape(t, 256) for y in (y_a, y_b, y_c, y_d)]
        x2 = _merge(x2, mod, g1, ys, w_gate[l].astype(BF16), b_gate[l][:, None, :], w_branch[l].astype(BF16),
                    w_out[l].astype(BF16), seq, tm)
        x2 = _ffn(x2, mod, norm2_g[l].reshape(1, d), w_ff1[l].astype(BF16), w_ff2[l].astype(BF16),
                  final_norm_g.reshape(1, d), l == depth - 1, seq, tm)
    return x2.reshape(b, seq, d)
```

```python
import functools
import math

import numpy as np
import jax
import jax.numpy as jnp
from jax import lax
from jax.experimental import pallas as pl
from jax.experimental.pallas import tpu as pltpu

F32 = jnp.float32
BF16 = jnp.bfloat16
HIGHEST = lax.Precision.HIGHEST

NEG = -1e30
BELOW_NEG = -3.0e38
NORM_EPS = 1e-6
LOG2E = 1.4426950408889634
BLK = 256
LANE = 128
SUBLANE = 8
HEAD = 64
N_HEADS = 4
VT_ROWS = HEAD + 16
DIFF_DH = 32
MOBA_TOPK = 3
DSA_TOPK = 256
IDX_DH = 32
N_IDX_HEADS = 4
POOL_WINDOWS = (2, 4, 8, 16)
POOL_HALO = 16
INT_MIN = -(2 ** 31)
VMEM_LIMIT = 56 * 1024 * 1024

_ALL_SLOPES = [float(np.float32(2.0 ** (-8.0 * i / 12))) for i in range(1, 13)]
SLOPES_A = _ALL_SLOPES[0::3]
SLOPES_B = _ALL_SLOPES[1::3]
SLOPES_D = _ALL_SLOPES[2::3]

ATT_W = 6 * 256
V_OFF = ATT_W
V_W = 3 * 256
PU_OFF = V_OFF + V_W
IDX_OFF = PU_OFF + 256
IN_W = IDX_OFF + 3 * LANE


def _key_of_neg():
    bits = int(np.array(NEG, np.float32).view(np.int32))
    return INT_MIN - bits


KEY_NEG = _key_of_neg()


def _nt(a, b, precision=None):
    return lax.dot_general(a, b, (((1,), (1,)), ((), ())), preferred_element_type=F32, precision=precision)


def _modulated_norm(x, g, shift, scale):
    ms = jnp.mean(x * x, axis=-1, keepdims=True)
    y = x * lax.rsqrt(ms + NORM_EPS) * g
    return y * (1.0 + scale) + shift


def _mod_kernel(c_ref, w_ref, b_ref, o_ref):
    c = c_ref[...]
    act = c * jax.nn.sigmoid(c)
    o_ref[0] = jnp.dot(act, w_ref[0], preferred_element_type=F32, precision=HIGHEST) + b_ref[0]


def _modulation(c, w_ada, b_ada):
    depth, d, n = w_ada.shape
    b = c.shape[0]
    tn = n // 4
    return pl.pallas_call(
        _mod_kernel,
        out_shape=jax.ShapeDtypeStruct((depth, b, n), F32),
        grid=(depth, n // tn),
        in_specs=[pl.BlockSpec((b, d), lambda l, j: (0, 0)),
                  pl.BlockSpec((1, d, tn), lambda l, j: (l, 0, j)),
                  pl.BlockSpec((1, 1, tn), lambda l, j: (l, 0, j))],
        out_specs=pl.BlockSpec((1, b, tn), lambda l, j: (l, 0, j)),
        compiler_params=pltpu.CompilerParams(vmem_limit_bytes=VMEM_LIMIT),
        name="adaln_mod",
    )(c, w_ada, b_ada.reshape(depth, 1, n))


def _in_kernel(x_ref, mod_ref, g_ref, w_ref, cs_ref, att_ref, vt_ref, pu_ref, idx_ref, iwt_ref, km_ref):
    mod = mod_ref[0]
    h = _modulated_norm(x_ref[...], g_ref[...], mod[0:1], mod[1:2])
    proj = jnp.dot(h.astype(BF16), w_ref[...], preferred_element_type=F32)
    att_ref[...] = (proj[:, :ATT_W] * cs_ref[...]).astype(BF16)
    pu_ref[...] = proj[:, PU_OFF:PU_OFF + 256]
    idx_ref[...] = proj[:, IDX_OFF:IDX_OFF + 2 * LANE].astype(BF16)
    for r in range(km_ref.shape[0]):
        rows = slice(r * BLK, (r + 1) * BLK)
        km_ref[r] = jnp.mean(proj[rows, 256:512], axis=0, keepdims=True)
        for g in range(3):
            vt = proj[rows, V_OFF + g * 256:V_OFF + (g + 1) * 256].T.astype(BF16)
            for hd in range(N_HEADS):
                base = (g * N_HEADS + hd) * VT_ROWS
                vt_ref[0, r, base:base + HEAD, :] = vt[hd * HEAD:(hd + 1) * HEAD]
                vt_ref[0, r, base + HEAD:base + VT_ROWS, :] = jnp.ones((VT_ROWS - HEAD, BLK), BF16)
        iwt_ref[0, r] = proj[rows, IDX_OFF + 2 * LANE:].T[0:SUBLANE, :]


def _in_proj(x2, mod, g, w_in_r, colscale, b, seq, tm):
    t, d = x2.shape
    nb = tm // BLK
    per_seq = seq // tm
    vt_rows = 3 * N_HEADS * VT_ROWS
    return pl.pallas_call(
        _in_kernel,
        out_shape=(jax.ShapeDtypeStruct((t, ATT_W), BF16),
                   jax.ShapeDtypeStruct((b, seq // BLK, vt_rows, BLK), BF16),
                   jax.ShapeDtypeStruct((t, 256), F32),
                   jax.ShapeDtypeStruct((t, 2 * LANE), BF16),
                   jax.ShapeDtypeStruct((b, seq // BLK, SUBLANE, BLK), F32),
                   jax.ShapeDtypeStruct((t // BLK, 1, 256), F32)),
        grid=(t // tm,),
        in_specs=[pl.BlockSpec((tm, d), lambda i: (i, 0)),
                  pl.BlockSpec((1, 6, d), lambda i: (i // per_seq, 0, 0)),
                  pl.BlockSpec((1, d), lambda i: (0, 0)),
                  pl.BlockSpec((d, IN_W), lambda i: (0, 0)),
                  pl.BlockSpec((1, ATT_W), lambda i: (0, 0))],
        out_specs=(pl.BlockSpec((tm, ATT_W), lambda i: (i, 0)),
                   pl.BlockSpec((1, nb, vt_rows, BLK), lambda i: (i // per_seq, i % per_seq, 0, 0)),
                   pl.BlockSpec((tm, 256), lambda i: (i, 0)),
                   pl.BlockSpec((tm, 2 * LANE), lambda i: (i, 0)),
                   pl.BlockSpec((1, nb, SUBLANE, BLK), lambda i: (i // per_seq, i % per_seq, 0, 0)),
                   pl.BlockSpec((nb, 1, 256), lambda i: (i, 0, 0))),
        compiler_params=pltpu.CompilerParams(vmem_limit_bytes=VMEM_LIMIT),
        name="in_proj",
    )(x2, mod, g, w_in_r, colscale)


N_BIAS_TERMS = 3


def _feature_masks(width):
    lane = lax.broadcasted_iota(jnp.int32, (BLK, 256), 1)
    return [(lane >= lo) & (lane < lo + width) for lo in range(0, 256, width)]


def _bias_lane(group):
    return LANE if group == 0 else 0


def _group_queries(q):
    lane = lax.broadcasted_iota(jnp.int32, (BLK, 256), 1)
    out = []
    for group in range(2):
        lo = _bias_lane(group)
        ones = jnp.where((lane >= lo) & (lane < lo + N_BIAS_TERMS), 1.0, 0.0).astype(q.dtype)
        out.append(jnp.where((lane < LANE) == (group == 0), q, ones))
    return out


def _stacked_scores(k_blk, masks, fills, q_groups):
    half = len(masks) // 2
    tiles = []
    for group in range(2):
        rows = []
        for n in range(group * half, (group + 1) * half):
            fill = jnp.zeros((), k_blk.dtype) if fills is None else fills[n]
            rows.append(jnp.where(masks[n], k_blk, fill))
        s = _nt(jnp.concatenate(rows, axis=0), q_groups[group])
        tiles += [s[n * BLK:(n + 1) * BLK] for n in range(half)]
    return tiles


def _causal_t():
    sl = lax.broadcasted_iota(jnp.int32, (BLK, BLK), 0)
    tl = lax.broadcasted_iota(jnp.int32, (BLK, BLK), 1)
    return tl - sl


def _init_state(m_ref, acc_ref):
    m_ref[...] = jnp.full(m_ref.shape, NEG, F32)
    acc_ref[...] = jnp.zeros(acc_ref.shape, F32)


def _flash_step(s, c, n, vt_h, m_ref, acc_ref, chosen=None):
    s3 = s.reshape(BLK // SUBLANE, SUBLANE, BLK)
    m_blk = jnp.max(s3, axis=0)
    for step in (4, 2, 1):
        m_blk = jnp.maximum(m_blk, pltpu.roll(m_blk, step, 0))
    m_old = m_ref[n]
    m_new = jnp.maximum(m_old, m_blk + c)
    shift = m_new - c
    if chosen is not None:
        m_new = jnp.where(chosen, m_new, m_old)
        shift = jnp.where(chosen, shift, -NEG)
    p = jnp.exp2(s3 - shift[None]).reshape(BLK, BLK)
    alpha = jnp.exp2(m_old - m_new)
    pv = jnp.dot(vt_h, p.astype(BF16), preferred_element_type=F32)
    groups = VT_ROWS // SUBLANE
    acc = acc_ref[n].reshape(groups, SUBLANE, BLK) * alpha[None] + pv.reshape(groups, SUBLANE, BLK)
    acc_ref[n] = acc.reshape(VT_ROWS, BLK)
    m_ref[n] = m_new


def _block_offset(slope, i, j):
    steps = jnp.full((1, 1), i - j, jnp.int32).astype(F32)
    return steps * (-slope * LOG2E * BLK)


def _for_past_blocks(i, scores, consume, group=2):
    def trip(t, carry):
        tiles = [scores(group * t + u) for u in range(group)]
        for u in range(group):
            consume(tiles[u], group * t + u)
        return carry

    lax.fori_loop(0, i // group, trip, 0)

    def single(j, carry):
        consume(scores(j), j)
        return carry

    lax.fori_loop((i // group) * group, i, single, 0)


def _normalised(acc_ref, n):
    acc = acc_ref[n]
    return acc[0:HEAD] / acc[HEAD:HEAD + 1]


def _write_heads(o_ref, outs):
    for pair in range(2):
        both = jnp.concatenate([outs[2 * pair], outs[2 * pair + 1]], axis=0)
        o_ref[0, :, pair * LANE:(pair + 1) * LANE] = both.T.astype(o_ref.dtype)


def _attn_specs(seq, q_col, k_col, v_group):
    nblk = seq // BLK
    return [pl.BlockSpec((1, BLK, 256), lambda bi, i: (bi, i, q_col)),
            pl.BlockSpec((1, seq, 256), lambda bi, i: (bi, 0, k_col)),
            pl.BlockSpec((1, nblk, N_HEADS * VT_ROWS, BLK), lambda bi, i: (bi, 0, v_group, 0))]


def _state_scratch(n):
    return [pltpu.VMEM((n, SUBLANE, BLK), F32), pltpu.VMEM((n, VT_ROWS, BLK), F32)]


def _head_rows(vt_blk, h):
    return vt_blk[h * VT_ROWS:(h + 1) * VT_ROWS, :]


def _moba_kernel(q_ref, k_ref, vt_ref, km_ref, fill_ref, o_ref, m_ref, acc_ref, sel_ref):
    i = pl.program_id(1)
    nb = km_ref.shape[1]
    masks = _feature_masks(HEAD)
    row = lax.broadcasted_iota(jnp.int32, (nb, BLK), 0)
    row_f = row.astype(F32)
    causal = _causal_t() >= 0
    _init_state(m_ref, acc_ref)
    q = q_ref[0]
    q_groups = _group_queries(q)

    km = km_ref[0]
    km_head = lax.broadcasted_iota(jnp.int32, (nb, 256), 1) // HEAD
    terms = []
    rest = jnp.concatenate([jnp.where(km_head == h, km, 0.0) for h in range(N_HEADS)], axis=0)
    for _ in range(3):
        part = rest.astype(BF16)
        terms.append(part)
        rest = rest - part.astype(F32)
    gates = _nt(jnp.concatenate(terms, axis=0), q)
    rows = N_HEADS * nb
    gates = gates[0:rows] + gates[rows:2 * rows] + gates[2 * rows:3 * rows]
    for h in range(N_HEADS):
        g = jnp.where(row < i, gates[h * nb:(h + 1) * nb], NEG)
        sel = jnp.zeros((nb, BLK), F32)
        for _ in range(MOBA_TOPK):
            mx = jnp.max(g, axis=0, keepdims=True)
            first = jnp.min(jnp.where(g == mx, row_f, 1e9), axis=0, keepdims=True)
            pick = row_f == first
            sel = jnp.where(pick, 1.0, sel)
            g = jnp.where(pick, BELOW_NEG, g)
        sel_ref[h] = jnp.where(row < i, sel, 0.0)

    def scores(j):
        start = pl.multiple_of(j * BLK, BLK)
        fills = [fill_ref[h] for h in range(N_HEADS)]
        return _stacked_scores(k_ref[0, pl.ds(start, BLK), :], masks, fills, q_groups)

    def consume(s_all, j):
        vt_j = vt_ref[0, j]
        for h in range(N_HEADS):
            chosen = sel_ref[h, pl.ds(j, 1), :] > 0.5
            _flash_step(s_all[h], _block_offset(SLOPES_A[h], i, j), h, _head_rows(vt_j, h), m_ref, acc_ref,
                        chosen)

    _for_past_blocks(i, scores, consume, group=4)
    s_own = scores(i)
    vt_own = vt_ref[0, i]
    for h in range(N_HEADS):
        _flash_step(jnp.where(causal, s_own[h], NEG), 0.0, h, _head_rows(vt_own, h), m_ref, acc_ref)
    _write_heads(o_ref, [_normalised(acc_ref, h) for h in range(N_HEADS)])


def _moba(att, vt, kmean_pad, bias, seq):
    b = att.shape[0]
    nb = kmean_pad.shape[1]
    return pl.pallas_call(
        _moba_kernel,
        out_shape=jax.ShapeDtypeStruct((b, seq, 256), BF16),
        grid=(b, seq // BLK),
        in_specs=_attn_specs(seq, 0, 1, 0) + [
            pl.BlockSpec((1, nb, 256), lambda bi, i: (bi, 0, 0)),
            pl.BlockSpec((N_HEADS, BLK, BLK), lambda bi, i: (0, 0, 0))],
        out_specs=pl.BlockSpec((1, BLK, 256), lambda bi, i: (bi, i, 0)),
        scratch_shapes=_state_scratch(N_HEADS) + [pltpu.VMEM((N_HEADS, nb, BLK), F32)],
        compiler_params=pltpu.CompilerParams(vmem_limit_bytes=VMEM_LIMIT),
        name="moba_attn",
    )(att, att, vt, kmean_pad, bias)


def _diff_kernel(lambda_init, q_ref, k_ref, vt_ref, fill_ref, dl_ref, g_ref, o_ref, m_ref, acc_ref):
    i = pl.program_id(1)
    masks = _feature_masks(DIFF_DH)
    causal = _causal_t() >= 0
    _init_state(m_ref, acc_ref)
    q_groups = _group_queries(q_ref[0])

    def scores(j):
        start = pl.multiple_of(j * BLK, BLK)
        fills = [fill_ref[n // 2] for n in range(2 * N_HEADS)]
        return _stacked_scores(k_ref[0, pl.ds(start, BLK), :], masks, fills, q_groups)

    def consume(s_all, j, c_of, mask):
        vt_j = vt_ref[0, j]
        for n in range(2 * N_HEADS):
            h = n // 2
            s = s_all[n]
            if mask is not None:
                s = jnp.where(mask, s, NEG)
            _flash_step(s, c_of(h), n, _head_rows(vt_j, h), m_ref, acc_ref)

    _for_past_blocks(i, scores, lambda s_all, j: consume(s_all, j, lambda h: _block_offset(SLOPES_B[h], i, j), None),
                     group=4)
    consume(scores(i), i, lambda h: 0.0, causal)

    dl = dl_ref[...]
    lam = (jnp.exp(jnp.sum(dl[0:1] * dl[1:2], axis=1, keepdims=True))
           - jnp.exp(jnp.sum(dl[2:3] * dl[3:4], axis=1, keepdims=True)) + lambda_init)
    lane = lax.broadcasted_iota(jnp.int32, (BLK, LANE), 1)
    for pair in range(2):
        both = []
        for h in (2 * pair, 2 * pair + 1):
            both.append(_normalised(acc_ref, 2 * h) - lam * _normalised(acc_ref, 2 * h + 1))
        o = jnp.concatenate(both, axis=0).T
        sq = o * o
        ms = jnp.where(lane < HEAD,
                       jnp.sum(jnp.where(lane < HEAD, sq, 0.0), axis=1, keepdims=True),
                       jnp.sum(jnp.where(lane >= HEAD, sq, 0.0), axis=1, keepdims=True)) * (1.0 / HEAD)
        y = o * lax.rsqrt(ms + NORM_EPS) * g_ref[...] * (1.0 - lambda_init)
        o_ref[0, :, pair * LANE:(pair + 1) * LANE] = y.astype(o_ref.dtype)


def _diff(att, vt, bias, diff_lambda, subln_g2, lambda_init, seq):
    b = att.shape[0]
    return pl.pallas_call(
        functools.partial(_diff_kernel, lambda_init),
        out_shape=jax.ShapeDtypeStruct((b, seq, 256), BF16),
        grid=(b, seq // BLK),
        in_specs=_attn_specs(seq, 2, 3, 1) + [
            pl.BlockSpec((N_HEADS, BLK, BLK), lambda bi, i: (0, 0, 0)),
            pl.BlockSpec((4, DIFF_DH), lambda bi, i: (0, 0)),
            pl.BlockSpec((1, LANE), lambda bi, i: (0, 0))],
        out_specs=pl.BlockSpec((1, BLK, 256), lambda bi, i: (bi, i, 0)),
        scratch_shapes=_state_scratch(2 * N_HEADS),
        compiler_params=pltpu.CompilerParams(vmem_limit_bytes=VMEM_LIMIT),
        name="diff_attn",
    )(att, att, vt, bias, diff_lambda, subln_g2)


def _dsa_kernel(topk, q_ref, k_ref, vt_ref, iq_ref, ik_ref, iwt_ref, fill_ref, tri_ref, o_ref,
                m_ref, acc_ref, key_ref, taken_ref):
    i = pl.program_id(1)
    dist = _causal_t()
    _init_state(m_ref, acc_ref)

    iq = iq_ref[0]
    lane = lax.broadcasted_iota(jnp.int32, (BLK, LANE), 1)
    idx_masks = [(lane >= h * IDX_DH) & (lane < (h + 1) * IDX_DH) for h in range(N_IDX_HEADS)]
    iwt = iwt_ref[0, 0]
    wrow = [iwt[h:h + 1, :] * (N_IDX_HEADS ** -0.5) for h in range(N_IDX_HEADS)]

    def score_block(j, carry):
        start = pl.multiple_of(j * BLK, BLK)
        rel_all = _stacked_scores(ik_ref[0, pl.ds(start, BLK), :], idx_masks, None, (iq, iq))
        sc = jnp.zeros((BLK, BLK), F32)
        for h in range(N_IDX_HEADS):
            sc = sc + jnp.maximum(rel_all[h] * (IDX_DH ** -0.5), 0.0) * wrow[h]
        limit = jnp.where(j < i, BLK, 0)
        sc = jnp.where(dist + limit >= 0, sc, NEG)
        bits = lax.bitcast_convert_type(sc, jnp.int32)
        key_ref[j] = jnp.where(bits < 0, INT_MIN - bits, bits)
        return carry

    lax.fori_loop(0, i + 1, score_block, 0)

    def count_where(pred):
        def blk(j, acc):
            hit = jnp.where(pred(key_ref[j]), 1.0, 0.0)
            return acc + jnp.sum(hit.reshape(BLK // SUBLANE, SUBLANE, BLK), axis=0)
        acc = lax.fori_loop(0, i + 1, blk, jnp.zeros((SUBLANE, BLK), F32))
        return jnp.sum(acc, axis=0, keepdims=True)

    def search(step, ans):
        cand = ans + jnp.left_shift(jnp.int32(1), 31 - step)
        cnt = count_where(lambda key: key >= cand)
        return jnp.where(cnt >= topk, cand, ans)

    thr = lax.fori_loop(0, 32, search, jnp.full((1, BLK), INT_MIN, jnp.int32))
    n_above = count_where(lambda key: key > thr)
    need = jnp.where(thr == KEY_NEG, 0.0, topk - n_above)

    q_groups = _group_queries(q_ref[0])
    masks = _feature_masks(HEAD)

    def scores(j):
        start = pl.multiple_of(j * BLK, BLK)
        fills = [fill_ref[h] for h in range(N_HEADS)]
        return _stacked_scores(k_ref[0, pl.ds(start, BLK), :], masks, fills, q_groups)

    taken_ref[...] = jnp.zeros((1, BLK), F32)

    def attend(s_all, j):
        vt_j = vt_ref[0, j]
        key = key_ref[j]
        taken = taken_ref[...]
        tie = jnp.where(key == thr, 1.0, 0.0)
        rank = jnp.dot(tri_ref[...], tie.astype(BF16), preferred_element_type=F32)
        tie_ok = jnp.where(rank <= need - taken, tie, 0.0)
        chosen = jnp.where(key > thr, 1.0, tie_ok) > 0.5
        taken_ref[...] = taken + jnp.sum(tie, axis=0, keepdims=True)
        for h in range(N_HEADS):
            s = jnp.where(chosen, s_all[h], NEG)
            _flash_step(s, _block_offset(SLOPES_D[h], i, j), h, _head_rows(vt_j, h), m_ref, acc_ref)

    _for_past_blocks(i, scores, attend, group=4)
    attend(scores(i), i)
    _write_heads(o_ref, [_normalised(acc_ref, h) for h in range(N_HEADS)])


def _dsa(att, vt, idx, iwt, bias, tri, seq):
    b = att.shape[0]
    nblk = seq // BLK
    topk = float(min(DSA_TOPK, seq // 4))
    return pl.pallas_call(
        functools.partial(_dsa_kernel, topk),
        out_shape=jax.ShapeDtypeStruct((b, seq, 256), BF16),
        grid=(b, nblk),
        in_specs=_attn_specs(seq, 4, 5, 2) + [
            pl.BlockSpec((1, BLK, LANE), lambda bi, i: (bi, i, 0)),
            pl.BlockSpec((1, seq, LANE), lambda bi, i: (bi, 0, 1)),
            pl.BlockSpec((1, 1, SUBLANE, BLK), lambda bi, i: (bi, i, 0, 0)),
            pl.BlockSpec((N_HEADS, BLK, BLK), lambda bi, i: (0, 0, 0)),
            pl.BlockSpec((BLK, BLK), lambda bi, i: (0, 0))],
        out_specs=pl.BlockSpec((1, BLK, 256), lambda bi, i: (bi, i, 0)),
        scratch_shapes=_state_scratch(N_HEADS) + [pltpu.VMEM((nblk, BLK, BLK), jnp.int32),
                                                  pltpu.VMEM((1, BLK), F32)],
        compiler_params=pltpu.CompilerParams(vmem_limit_bytes=VMEM_LIMIT),
        name="dsa_attn",
    )(att, att, vt, idx, idx, iwt, bias, tri)


def _pool_kernel(tp, u_ref, halo_ref, w_ref, ps_ref, o_ref, a_ref, b_ref):
    i = pl.program_id(1)
    u = u_ref[0]
    body = POOL_HALO + tp
    a_ref[0:POOL_HALO] = jnp.zeros((POOL_HALO, 256), F32)
    b_ref[0:POOL_HALO] = jnp.zeros((POOL_HALO, 256), F32)
    a_ref[POOL_HALO:2 * POOL_HALO] = jnp.where(i > 0, halo_ref[0], 0.0)
    a_ref[2 * POOL_HALO:] = u
    group = jnp.right_shift(lax.broadcasted_iota(jnp.int32, (tp, 256), 1), 6)
    src, dst = a_ref, b_ref
    win = jnp.zeros((tp, 256), F32)
    shift = 1
    for g, w in enumerate(POOL_WINDOWS):
        while shift < w:
            dst[POOL_HALO:] = src[POOL_HALO:] + src[POOL_HALO - shift:POOL_HALO - shift + body]
            src, dst = dst, src
            shift *= 2
        win = jnp.where(group == g, src[2 * POOL_HALO:], win)
    t = i * tp + lax.broadcasted_iota(jnp.int32, (tp, 256), 0)
    width = jnp.where(group == 0, 2, jnp.where(group == 1, 4, jnp.where(group == 2, 8, 16)))
    cnt = jnp.minimum(t + 1, width).astype(F32)
    d = win / cnt - u
    y = jnp.dot(d.astype(BF16), w_ref[...], preferred_element_type=F32) * ps_ref[...]
    o_ref[0] = y.astype(o_ref.dtype)


def _pool(pu, w_pool_bd, pool_scale, tp):
    b, seq, _ = pu.shape
    per = tp // POOL_HALO
    return pl.pallas_call(
        functools.partial(_pool_kernel, tp),
        out_shape=jax.ShapeDtypeStruct((b, seq, 256), BF16),
        grid=(b, seq // tp),
        in_specs=[pl.BlockSpec((1, tp, 256), lambda bi, i: (bi, i, 0)),
                  pl.BlockSpec((1, POOL_HALO, 256), lambda bi, i: (bi, jnp.maximum(i * per - 1, 0), 0)),
                  pl.BlockSpec((256, 256), lambda bi, i: (0, 0)),
                  pl.BlockSpec((1, 256), lambda bi, i: (0, 0))],
        out_specs=pl.BlockSpec((1, tp, 256), lambda bi, i: (bi, i, 0)),
        scratch_shapes=[pltpu.VMEM((tp + 2 * POOL_HALO, 256), F32),
                        pltpu.VMEM((tp + 2 * POOL_HALO, 256), F32)],
        compiler_params=pltpu.CompilerParams(vmem_limit_bytes=VMEM_LIMIT),
        name="pool_mixer",
    )(pu, pu, w_pool_bd, pool_scale)


def _merge_kernel(x_ref, mod_ref, g_ref, ya_ref, yb_ref, yc_ref, yd_ref, wg_ref, bg_ref, wb_ref, wo_ref, o_ref):
    mod = mod_ref[0]
    x = x_ref[...]
    hb = _modulated_norm(x, g_ref[...], mod[0:1], mod[1:2]).astype(BF16)
    merged = jnp.zeros(x.shape, F32)
    for n, y_ref in enumerate((ya_ref, yb_ref, yc_ref, yd_ref)):
        gate = jax.nn.sigmoid(jnp.dot(hb, wg_ref[n], preferred_element_type=F32) + bg_ref[n])
        merged = merged + gate * jnp.dot(y_ref[...], wb_ref[n], preferred_element_type=F32)
    o_ref[...] = x + mod[2:3] * jnp.dot(merged.astype(BF16), wo_ref[...], preferred_element_type=F32)


def _resident(shape):
    zeros = (0,) * len(shape)
    return pl.BlockSpec(shape, lambda i: zeros, pipeline_mode=pl.Buffered(1))


def _merge(x2, mod, g, ys, w_gate, b_gate, w_branch, w_out, seq, tm):
    t, d = x2.shape
    tok = lambda w: pl.BlockSpec((tm, w), lambda i: (i, 0))
    return pl.pallas_call(
        _merge_kernel,
        out_shape=jax.ShapeDtypeStruct((t, d), F32),
        grid=(t // tm,),
        in_specs=[tok(d),
                  pl.BlockSpec((1, 6, d), lambda i: (i * tm // seq, 0, 0)),
                  _resident((1, d)),
                  tok(256), tok(256), tok(256), tok(256),
                  _resident(w_gate.shape), _resident(b_gate.shape), _resident(w_branch.shape),
                  _resident(w_out.shape)],
        out_specs=tok(d),
        compiler_params=pltpu.CompilerParams(vmem_limit_bytes=VMEM_LIMIT),
        name="gated_merge",
    )(x2, mod, g, *ys, w_gate, b_gate, w_branch, w_out)


def _ffn_kernel(final, x_ref, mod_ref, g_ref, w1_ref, w2_ref, fg_ref, o_ref):
    mod = mod_ref[0]
    x = x_ref[...]
    hb = _modulated_norm(x, g_ref[...], mod[3:4], mod[4:5]).astype(BF16)
    d = x.shape[1]
    ff = jnp.zeros(x.shape, F32)
    for c in range(w1_ref.shape[1] // d):
        a = jnp.maximum(jnp.dot(hb, w1_ref[:, c * d:(c + 1) * d], preferred_element_type=F32), 0.0)
        ff = ff + jnp.dot((a * a).astype(BF16), w2_ref[c * d:(c + 1) * d, :], preferred_element_type=F32)
    out = x + mod[5:6] * ff
    if final:
        ms = jnp.mean(out * out, axis=-1, keepdims=True)
        out = out * lax.rsqrt(ms + NORM_EPS) * fg_ref[...]
    o_ref[...] = out


def _ffn(x2, mod, g, w1, w2, final_g, final, seq, tm):
    t, d = x2.shape
    tok = pl.BlockSpec((tm, d), lambda i: (i, 0))
    return pl.pallas_call(
        functools.partial(_ffn_kernel, final),
        out_shape=jax.ShapeDtypeStruct((t, d), F32),
        grid=(t // tm,),
        in_specs=[tok,
                  pl.BlockSpec((1, 6, d), lambda i: (i * tm // seq, 0, 0)),
                  _resident((1, d)), _resident(w1.shape), _resident(w2.shape), _resident((1, d))],
        out_specs=tok,
        compiler_params=pltpu.CompilerParams(vmem_limit_bytes=VMEM_LIMIT),
        name="ffn",
    )(x2, mod, g, w1, w2, final_g)


def _alibi_tables(slopes):
    out = np.zeros((len(slopes), BLK, 256), np.float32)
    for h, m in enumerate(slopes):
        rest = (m * LOG2E * np.arange(BLK, dtype=np.float64)).astype(np.float32)
        lo = _bias_lane(h // (len(slopes) // 2))
        for term in range(N_BIAS_TERMS):
            piece = rest.astype(BF16).astype(np.float32)
            out[h, :, lo + term] = piece
            rest = rest - piece
    return jnp.asarray(out.astype(BF16))


def _reorder_w_in(w_in):
    o = np.cumsum([0, 256, 256, 256, 256, 256, 256, 256, 256, 256, 256, 128, 32, 4])
    piece = lambda n: w_in[:, o[n]:o[n + 1]]
    d = w_in.shape[0]
    cols = [piece(n) for n in (0, 1, 3, 4, 7, 8, 2, 5, 9, 6, 10)]
    cols += [piece(11)] * (LANE // IDX_DH)
    cols += [piece(12), jnp.zeros((d, LANE - N_IDX_HEADS), w_in.dtype)]
    return jnp.concatenate(cols, axis=1)


def _q_colscale():
    s = np.ones((1, ATT_W), np.float32)
    s[0, 0:256] = HEAD ** -0.5 * LOG2E
    s[0, 512:768] = DIFF_DH ** -0.5 * LOG2E
    s[0, 1024:1280] = HEAD ** -0.5 * LOG2E
    return jnp.asarray(s)


def _block_diag(w_pool):
    g, c, _ = w_pool.shape
    out = jnp.zeros((g * c, g * c), w_pool.dtype)
    for n in range(g):
        out = out.at[n * c:(n + 1) * c, n * c:(n + 1) * c].set(w_pool[n])
    return out


def kernel(x, c, w_ada, b_ada, norm1_g, w_in, diff_lambda, diff_subln_g, w_pool, pool_scale, w_gate, b_gate,
           w_branch, w_out, norm2_g, w_ff1, w_ff2, final_norm_g):
    b, seq, d = x.shape
    depth = w_ada.shape[0]
    t = b * seq
    tm = 512
    nblk = seq // BLK
    nb_pad = -(-nblk // SUBLANE) * SUBLANE
    assert seq % tm == 0
    mods = _modulation(c, w_ada, b_ada).reshape(depth, b, 6, d)
    bias_a, bias_b, bias_d = _alibi_tables(SLOPES_A), _alibi_tables(SLOPES_B), _alibi_tables(SLOPES_D)
    tri = jnp.asarray(np.tril(np.ones((BLK, BLK), np.float32))).astype(BF16)
    colscale = _q_colscale()
    x2 = x.reshape(t, d)
    for l in range(depth):
        mod = mods[l]
        g1 = norm1_g[l].reshape(1, d)
        att, vt, pu, idx, iwt, kmean = _in_proj(x2, mod, g1, _reorder_w_in(w_in[l]).astype(BF16), colscale,
                                                b, seq, tm)
        att = att.reshape(b, seq, ATT_W)
        kmean_pad = jnp.pad(kmean.reshape(b, nblk, 256), ((0, 0), (0, nb_pad - nblk), (0, 0)))
        y_a = _moba(att, vt, kmean_pad, bias_a, seq)
        lambda_init = 0.8 - 0.6 * math.exp(-0.3 * l)
        y_b = _diff(att, vt, bias_b, diff_lambda[l], jnp.tile(diff_subln_g[l], 2).reshape(1, LANE), lambda_init, seq)
        y_c = _pool(pu.reshape(b, seq, 256), _block_diag(w_pool[l]).astype(BF16), pool_scale[l].reshape(1, 256), tm)
        y_d = _dsa(att, vt, idx.reshape(b, seq, 2 * LANE), iwt, bias_d, tri, seq)
        ys = [y.reshape(t, 256) for y in (y_a, y_b, y_c, y_d)]
        x2 = _merge(x2, mod, g1, ys, w_gate[l].astype(BF16), b_gate[l][:, None, :], w_branch[l].astype(BF16),
                    w_out[l].astype(BF16), seq, tm)
        x2 = _ffn(x2, mod, norm2_g[l].reshape(1, d), w_ff1[l].astype(BF16), w_ff2[l].astype(BF16),
                  final_norm_g.reshape(1, d), l == depth - 1, seq, tm)
    return x2.reshape(b, seq, d)
```

```python
import functools
import math

import numpy as np
import jax
import jax.numpy as jnp
from jax import lax
from jax.experimental import pallas as pl
from jax.experimental.pallas import tpu as pltpu

F32 = jnp.float32
BF16 = jnp.bfloat16
HIGHEST = lax.Precision.HIGHEST

NEG = -1e30
BELOW_NEG = -3.0e38
NORM_EPS = 1e-6
LOG2E = 1.4426950408889634
BLK = 256
LANE = 128
SUBLANE = 8
HEAD = 64
N_HEADS = 4
VT_ROWS = HEAD + 16
DIFF_DH = 32
MOBA_TOPK = 3
DSA_TOPK = 256
IDX_DH = 32
N_IDX_HEADS = 4
POOL_WINDOWS = (2, 4, 8, 16)
POOL_HALO = 16
INT_MIN = -(2 ** 31)
HALF_BITS = 16
HALF_MASK = 2 ** HALF_BITS - 1
INT16_MIN = -(2 ** (HALF_BITS - 1))
PACKED_SUBLANE = 16
VMEM_LIMIT = 56 * 1024 * 1024

_ALL_SLOPES = [float(np.float32(2.0 ** (-8.0 * i / 12))) for i in range(1, 13)]
SLOPES_A = _ALL_SLOPES[0::3]
SLOPES_B = _ALL_SLOPES[1::3]
SLOPES_D = _ALL_SLOPES[2::3]

ATT_W = 6 * 256
V_OFF = ATT_W
V_W = 3 * 256
PU_OFF = V_OFF + V_W
IDX_OFF = PU_OFF + 256
IN_W = IDX_OFF + 3 * LANE


def _key_of_neg():
    bits = int(np.array(NEG, np.float32).view(np.int32))
    return INT_MIN - bits


KEY_NEG = _key_of_neg()


def _nt(a, b, precision=None):
    return lax.dot_general(a, b, (((1,), (1,)), ((), ())), preferred_element_type=F32, precision=precision)


def _modulated_norm(x, g, shift, scale):
    ms = jnp.mean(x * x, axis=-1, keepdims=True)
    y = x * lax.rsqrt(ms + NORM_EPS) * g
    return y * (1.0 + scale) + shift


def _mod_kernel(c_ref, w_ref, b_ref, o_ref):
    c = c_ref[...]
    act = c * jax.nn.sigmoid(c)
    o_ref[0] = jnp.dot(act, w_ref[0], preferred_element_type=F32, precision=HIGHEST) + b_ref[0]


def _modulation(c, w_ada, b_ada):
    depth, d, n = w_ada.shape
    b = c.shape[0]
    tn = n // 4
    return pl.pallas_call(
        _mod_kernel,
        out_shape=jax.ShapeDtypeStruct((depth, b, n), F32),
        grid=(depth, n // tn),
        in_specs=[pl.BlockSpec((b, d), lambda l, j: (0, 0)),
                  pl.BlockSpec((1, d, tn), lambda l, j: (l, 0, j)),
                  pl.BlockSpec((1, 1, tn), lambda l, j: (l, 0, j))],
        out_specs=pl.BlockSpec((1, b, tn), lambda l, j: (l, 0, j)),
        compiler_params=pltpu.CompilerParams(vmem_limit_bytes=VMEM_LIMIT),
        name="adaln_mod",
    )(c, w_ada, b_ada.reshape(depth, 1, n))


def _in_kernel(x_ref, mod_ref, g_ref, w_ref, cs_ref, att_ref, vt_ref, pu_ref, idx_ref, iwt_ref, km_ref):
    mod = mod_ref[0]
    h = _modulated_norm(x_ref[...], g_ref[...], mod[0:1], mod[1:2])
    proj = jnp.dot(h.astype(BF16), w_ref[...], preferred_element_type=F32)
    att_ref[...] = (proj[:, :ATT_W] * cs_ref[...]).astype(BF16)
    pu_ref[...] = proj[:, PU_OFF:PU_OFF + 256]
    idx_ref[...] = proj[:, IDX_OFF:IDX_OFF + 2 * LANE].astype(BF16)
    for r in range(km_ref.shape[0]):
        rows = slice(r * BLK, (r + 1) * BLK)
        km_ref[r] = jnp.mean(proj[rows, 256:512], axis=0, keepdims=True)
        for g in range(3):
            vt = proj[rows, V_OFF + g * 256:V_OFF + (g + 1) * 256].T.astype(BF16)
            for hd in range(N_HEADS):
                base = (g * N_HEADS + hd) * VT_ROWS
                vt_ref[0, r, base:base + HEAD, :] = vt[hd * HEAD:(hd + 1) * HEAD]
                vt_ref[0, r, base + HEAD:base + VT_ROWS, :] = jnp.ones((VT_ROWS - HEAD, BLK), BF16)
        iwt_ref[0, r] = proj[rows, IDX_OFF + 2 * LANE:].T[0:SUBLANE, :]


def _in_proj(x2, mod, g, w_in_r, colscale, b, seq, tm):
    t, d = x2.shape
    nb = tm // BLK
    per_seq = seq // tm
    vt_rows = 3 * N_HEADS * VT_ROWS
    return pl.pallas_call(
        _in_kernel,
        out_shape=(jax.ShapeDtypeStruct((t, ATT_W), BF16),
                   jax.ShapeDtypeStruct((b, seq // BLK, vt_rows, BLK), BF16),
                   jax.ShapeDtypeStruct((t, 256), F32),
                   jax.ShapeDtypeStruct((t, 2 * LANE), BF16),
                   jax.ShapeDtypeStruct((b, seq // BLK, SUBLANE, BLK), F32),
                   jax.ShapeDtypeStruct((t // BLK, 1, 256), F32)),
        grid=(t // tm,),
        in_specs=[pl.BlockSpec((tm, d), lambda i: (i, 0)),
                  pl.BlockSpec((1, 6, d), lambda i: (i // per_seq, 0, 0)),
                  pl.BlockSpec((1, d), lambda i: (0, 0)),
                  pl.BlockSpec((d, IN_W), lambda i: (0, 0)),
                  pl.BlockSpec((1, ATT_W), lambda i: (0, 0))],
        out_specs=(pl.BlockSpec((tm, ATT_W), lambda i: (i, 0)),
                   pl.BlockSpec((1, nb, vt_rows, BLK), lambda i: (i // per_seq, i % per_seq, 0, 0)),
                   pl.BlockSpec((tm, 256), lambda i: (i, 0)),
                   pl.BlockSpec((tm, 2 * LANE), lambda i: (i, 0)),
                   pl.BlockSpec((1, nb, SUBLANE, BLK), lambda i: (i // per_seq, i % per_seq, 0, 0)),
                   pl.BlockSpec((nb, 1, 256), lambda i: (i, 0, 0))),
        compiler_params=pltpu.CompilerParams(vmem_limit_bytes=VMEM_LIMIT),
        name="in_proj",
    )(x2, mod, g, w_in_r, colscale)


N_BIAS_TERMS = 3


def _feature_masks(width):
    lane = lax.broadcasted_iota(jnp.int32, (BLK, 256), 1)
    return [(lane >= lo) & (lane < lo + width) for lo in range(0, 256, width)]


def _bias_lane(group):
    return LANE if group == 0 else 0


def _group_queries(q):
    lane = lax.broadcasted_iota(jnp.int32, (BLK, 256), 1)
    out = []
    for group in range(2):
        lo = _bias_lane(group)
        ones = jnp.where((lane >= lo) & (lane < lo + N_BIAS_TERMS), 1.0, 0.0).astype(q.dtype)
        out.append(jnp.where((lane < LANE) == (group == 0), q, ones))
    return out


def _stacked_scores(k_blk, masks, fills, q_groups):
    half = len(masks) // 2
    tiles = []
    for group in range(2):
        rows = []
        for n in range(group * half, (group + 1) * half):
            fill = jnp.zeros((), k_blk.dtype) if fills is None else fills[n]
            rows.append(jnp.where(masks[n], k_blk, fill))
        s = _nt(jnp.concatenate(rows, axis=0), q_groups[group])
        tiles += [s[n * BLK:(n + 1) * BLK] for n in range(half)]
    return tiles


def _causal_t():
    sl = lax.broadcasted_iota(jnp.int32, (BLK, BLK), 0)
    tl = lax.broadcasted_iota(jnp.int32, (BLK, BLK), 1)
    return tl - sl


def _init_state(m_ref, acc_ref):
    m_ref[...] = jnp.full(m_ref.shape, NEG, F32)
    acc_ref[...] = jnp.zeros(acc_ref.shape, F32)


def _flash_step(s, c, n, vt_h, m_ref, acc_ref, chosen=None):
    s3 = s.reshape(BLK // SUBLANE, SUBLANE, BLK)
    m_blk = jnp.max(s3, axis=0)
    for step in (4, 2, 1):
        m_blk = jnp.maximum(m_blk, pltpu.roll(m_blk, step, 0))
    m_old = m_ref[n]
    m_new = jnp.maximum(m_old, m_blk + c)
    shift = m_new - c
    if chosen is not None:
        m_new = jnp.where(chosen, m_new, m_old)
        shift = jnp.where(chosen, shift, -NEG)
    p = jnp.exp2(s3 - shift[None]).reshape(BLK, BLK)
    alpha = jnp.exp2(m_old - m_new)
    pv = jnp.dot(vt_h, p.astype(BF16), preferred_element_type=F32)
    groups = VT_ROWS // SUBLANE
    acc = acc_ref[n].reshape(groups, SUBLANE, BLK) * alpha[None] + pv.reshape(groups, SUBLANE, BLK)
    acc_ref[n] = acc.reshape(VT_ROWS, BLK)
    m_ref[n] = m_new


def _block_offset(slope, i, j):
    steps = jnp.full((1, 1), i - j, jnp.int32).astype(F32)
    return steps * (-slope * LOG2E * BLK)


def _for_blocks(n, scores, consume, consume_last=None):
    def run(first, count, last=None):
        tiles = [scores(first + u) for u in range(count)]
        for u in range(count):
            (last if last is not None and u == count - 1 else consume)(tiles[u], first + u)

    def trip(t, carry):
        run(4 * t, 4)
        return carry

    held = 0 if consume_last is None else jnp.minimum(n, 2)
    m = n - held
    lax.fori_loop(0, m // 4, trip, 0)
    rest = m % 4

    @pl.when(rest >= 2)
    def _():
        run(m - rest, 2)

    @pl.when(rest % 2 == 1)
    def _():
        run(m - 1, 1)

    if consume_last is not None:
        @pl.when(n >= 2)
        def _():
            run(n - 2, 2, consume_last)

        @pl.when(n == 1)
        def _():
            run(0, 1, consume_last)


def _normalised(acc_ref, n):
    acc = acc_ref[n]
    return acc[0:HEAD] / acc[HEAD:HEAD + 1]


def _write_heads(o_ref, outs):
    for pair in range(2):
        both = jnp.concatenate([outs[2 * pair], outs[2 * pair + 1]], axis=0)
        o_ref[0, :, pair * LANE:(pair + 1) * LANE] = both.T.astype(o_ref.dtype)


def _attn_specs(seq, q_col, k_col, v_group):
    nblk = seq // BLK
    return [pl.BlockSpec((1, BLK, 256), lambda bi, i: (bi, i, q_col)),
            pl.BlockSpec((1, seq, 256), lambda bi, i: (bi, 0, k_col)),
            pl.BlockSpec((1, nblk, N_HEADS * VT_ROWS, BLK), lambda bi, i: (bi, 0, v_group, 0))]


def _state_scratch(n):
    return [pltpu.VMEM((n, SUBLANE, BLK), F32), pltpu.VMEM((n, VT_ROWS, BLK), F32)]


def _head_rows(vt_blk, h):
    return vt_blk[h * VT_ROWS:(h + 1) * VT_ROWS, :]


def _moba_kernel(q_ref, k_ref, vt_ref, km_ref, fill_ref, o_ref, m_ref, acc_ref, sel_ref):
    i = pl.program_id(1)
    nb = km_ref.shape[1]
    masks = _feature_masks(HEAD)
    row = lax.broadcasted_iota(jnp.int32, (nb, BLK), 0)
    row_f = row.astype(F32)
    causal = _causal_t() >= 0
    _init_state(m_ref, acc_ref)
    q = q_ref[0]
    q_groups = _group_queries(q)

    km = km_ref[0]
    km_head = lax.broadcasted_iota(jnp.int32, (nb, 256), 1) // HEAD
    terms = []
    rest = jnp.concatenate([jnp.where(km_head == h, km, 0.0) for h in range(N_HEADS)], axis=0)
    for _ in range(3):
        part = rest.astype(BF16)
        terms.append(part)
        rest = rest - part.astype(F32)
    gates = _nt(jnp.concatenate(terms, axis=0), q)
    rows = N_HEADS * nb
    gates = gates[0:rows] + gates[rows:2 * rows] + gates[2 * rows:3 * rows]
    for h in range(N_HEADS):
        g = jnp.where(row < i, gates[h * nb:(h + 1) * nb], NEG)
        sel = jnp.zeros((nb, BLK), F32)
        for _ in range(MOBA_TOPK):
            mx = jnp.max(g, axis=0, keepdims=True)
            first = jnp.min(jnp.where(g == mx, row_f, 1e9), axis=0, keepdims=True)
            pick = row_f == first
            sel = jnp.where(pick, 1.0, sel)
            g = jnp.where(pick, BELOW_NEG, g)
        sel_ref[h] = jnp.where(row < i, sel, 0.0)

    def scores(j):
        start = pl.multiple_of(j * BLK, BLK)
        fills = [fill_ref[h] for h in range(N_HEADS)]
        return _stacked_scores(k_ref[0, pl.ds(start, BLK), :], masks, fills, q_groups)

    def consume(s_all, j):
        vt_j = vt_ref[0, j]
        for h in range(N_HEADS):
            chosen = sel_ref[h, pl.ds(j, 1), :] > 0.5
            _flash_step(s_all[h], _block_offset(SLOPES_A[h], i, j), h, _head_rows(vt_j, h), m_ref, acc_ref,
                        chosen)

    def consume_own(s_own, j):
        vt_own = vt_ref[0, j]
        for h in range(N_HEADS):
            _flash_step(jnp.where(causal, s_own[h], NEG), 0.0, h, _head_rows(vt_own, h), m_ref, acc_ref)

    _for_blocks(i + 1, scores, consume, consume_own)
    _write_heads(o_ref, [_normalised(acc_ref, h) for h in range(N_HEADS)])


def _moba(att, vt, kmean_pad, bias, seq):
    b = att.shape[0]
    nb = kmean_pad.shape[1]
    return pl.pallas_call(
        _moba_kernel,
        out_shape=jax.ShapeDtypeStruct((b, seq, 256), BF16),
        grid=(b, seq // BLK),
        in_specs=_attn_specs(seq, 0, 1, 0) + [
            pl.BlockSpec((1, nb, 256), lambda bi, i: (bi, 0, 0)),
            pl.BlockSpec((N_HEADS, BLK, BLK), lambda bi, i: (0, 0, 0))],
        out_specs=pl.BlockSpec((1, BLK, 256), lambda bi, i: (bi, i, 0)),
        scratch_shapes=_state_scratch(N_HEADS) + [pltpu.VMEM((N_HEADS, nb, BLK), F32)],
        compiler_params=pltpu.CompilerParams(vmem_limit_bytes=VMEM_LIMIT),
        name="moba_attn",
    )(att, att, vt, kmean_pad, bias)


def _diff_kernel(lambda_init, q_ref, k_ref, vt_ref, fill_ref, dl_ref, g_ref, o_ref, m_ref, acc_ref):
    i = pl.program_id(1)
    masks = _feature_masks(DIFF_DH)
    causal = _causal_t() >= 0
    _init_state(m_ref, acc_ref)
    q_groups = _group_queries(q_ref[0])

    def scores(j):
        start = pl.multiple_of(j * BLK, BLK)
        fills = [fill_ref[n // 2] for n in range(2 * N_HEADS)]
        return _stacked_scores(k_ref[0, pl.ds(start, BLK), :], masks, fills, q_groups)

    def consume(s_all, j, c_of, mask):
        vt_j = vt_ref[0, j]
        for n in range(2 * N_HEADS):
            h = n // 2
            s = s_all[n]
            if mask is not None:
                s = jnp.where(mask, s, NEG)
            _flash_step(s, c_of(h), n, _head_rows(vt_j, h), m_ref, acc_ref)

    _for_blocks(i + 1, scores,
                lambda s_all, j: consume(s_all, j, lambda h: _block_offset(SLOPES_B[h], i, j), None),
                lambda s_all, j: consume(s_all, j, lambda h: 0.0, causal))

    dl = dl_ref[...]
    lam = (jnp.exp(jnp.sum(dl[0:1] * dl[1:2], axis=1, keepdims=True))
           - jnp.exp(jnp.sum(dl[2:3] * dl[3:4], axis=1, keepdims=True)) + lambda_init)
    lane = lax.broadcasted_iota(jnp.int32, (BLK, LANE), 1)
    for pair in range(2):
        both = []
        for h in (2 * pair, 2 * pair + 1):
            both.append(_normalised(acc_ref, 2 * h) - lam * _normalised(acc_ref, 2 * h + 1))
        o = jnp.concatenate(both, axis=0).T
        sq = o * o
        ms = jnp.where(lane < HEAD,
                       jnp.sum(jnp.where(lane < HEAD, sq, 0.0), axis=1, keepdims=True),
                       jnp.sum(jnp.where(lane >= HEAD, sq, 0.0), axis=1, keepdims=True)) * (1.0 / HEAD)
        y = o * lax.rsqrt(ms + NORM_EPS) * g_ref[...] * (1.0 - lambda_init)
        o_ref[0, :, pair * LANE:(pair + 1) * LANE] = y.astype(o_ref.dtype)


def _diff(att, vt, bias, diff_lambda, subln_g2, lambda_init, seq):
    b = att.shape[0]
    return pl.pallas_call(
        functools.partial(_diff_kernel, lambda_init),
        out_shape=jax.ShapeDtypeStruct((b, seq, 256), BF16),
        grid=(b, seq // BLK),
        in_specs=_attn_specs(seq, 2, 3, 1) + [
            pl.BlockSpec((N_HEADS, BLK, BLK), lambda bi, i: (0, 0, 0)),
            pl.BlockSpec((4, DIFF_DH), lambda bi, i: (0, 0)),
            pl.BlockSpec((1, LANE), lambda bi, i: (0, 0))],
        out_specs=pl.BlockSpec((1, BLK, 256), lambda bi, i: (bi, i, 0)),
        scratch_shapes=_state_scratch(2 * N_HEADS),
        compiler_params=pltpu.CompilerParams(vmem_limit_bytes=VMEM_LIMIT),
        name="diff_attn",
    )(att, att, vt, bias, diff_lambda, subln_g2)


def _dsa_kernel(topk, q_ref, k_ref, vt_ref, iq_ref, ik_ref, iwt_ref, fill_ref, tri_ref, o_ref,
                m_ref, acc_ref, key_ref, hi_ref, lo_ref, taken_ref):
    i = pl.program_id(1)
    dist = _causal_t()
    _init_state(m_ref, acc_ref)

    iq = iq_ref[0]
    lane = lax.broadcasted_iota(jnp.int32, (BLK, LANE), 1)
    idx_masks = [(lane >= h * IDX_DH) & (lane < (h + 1) * IDX_DH) for h in range(N_IDX_HEADS)]
    iwt = iwt_ref[0, 0]
    wrow = [iwt[h:h + 1, :] * (N_IDX_HEADS ** -0.5 * IDX_DH ** -0.5) for h in range(N_IDX_HEADS)]

    def score_block(j, own):
        start = pl.multiple_of(j * BLK, BLK)
        rel_all = _stacked_scores(ik_ref[0, pl.ds(start, BLK), :], idx_masks, None, (iq, iq))
        sc = jnp.zeros((BLK, BLK), F32)
        for h in range(N_IDX_HEADS):
            sc = sc + jnp.maximum(rel_all[h], 0.0) * wrow[h]
        if own:
            sc = jnp.where(dist >= 0, sc, NEG)
        bits = lax.bitcast_convert_type(sc, jnp.int32)
        key = jnp.where(bits < 0, INT_MIN - bits, bits)
        key_ref[j] = key
        hi_ref[j] = lax.shift_right_arithmetic(key, HALF_BITS).astype(jnp.int16)
        lo_ref[j] = (jnp.bitwise_and(key, HALF_MASK) + INT16_MIN).astype(jnp.int16)

    def past_block(j, carry):
        score_block(j, False)
        return carry

    lax.fori_loop(0, i, past_block, 0)
    score_block(i, True)

    def count_where(ref, pred):
        rows = BLK // PACKED_SUBLANE

        def blk(j, acc):
            hit = jnp.where(pred(ref[j]), jnp.int16(1), jnp.int16(0))
            groups = hit.reshape(rows, PACKED_SUBLANE, BLK)
            parts = [groups[r] for r in range(rows)]
            while len(parts) > 1:
                parts = [a + b for a, b in zip(parts[0::2], parts[1::2])]
            return acc + parts[0].astype(jnp.int32)

        acc = lax.fori_loop(0, i + 1, blk, jnp.zeros((PACKED_SUBLANE, BLK), jnp.int32))
        return jnp.sum(acc, axis=0, keepdims=True).astype(F32)

    def half_search(ref, base):
        def step(n, ans):
            cand = ans + jnp.left_shift(jnp.int32(1), HALF_BITS - 1 - n)
            cand16 = cand.astype(jnp.int16)
            cnt = base + count_where(ref, lambda half: half >= cand16)
            return jnp.where(cnt >= topk, cand, ans)

        return lax.fori_loop(0, HALF_BITS, step, jnp.full((1, BLK), INT16_MIN, jnp.int32))

    thr_hi = half_search(hi_ref, 0.0)
    thr_hi16 = thr_hi.astype(jnp.int16)
    above_hi = count_where(hi_ref, lambda half: half > thr_hi16)

    def keep_bucket(j, carry):
        lo_ref[j] = jnp.where(hi_ref[j] == thr_hi16, lo_ref[j], jnp.int16(INT16_MIN))
        return carry

    lax.fori_loop(0, i + 1, keep_bucket, 0)
    thr_lo = half_search(lo_ref, above_hi)
    thr_lo16 = thr_lo.astype(jnp.int16)
    n_above = above_hi + count_where(lo_ref, lambda half: half > thr_lo16)
    thr = jnp.left_shift(thr_hi, HALF_BITS) + (thr_lo - INT16_MIN)
    need = jnp.where(thr == KEY_NEG, 0.0, topk - n_above)

    q_groups = _group_queries(q_ref[0])
    masks = _feature_masks(HEAD)

    def scores(j):
        start = pl.multiple_of(j * BLK, BLK)
        fills = [fill_ref[h] for h in range(N_HEADS)]
        return _stacked_scores(k_ref[0, pl.ds(start, BLK), :], masks, fills, q_groups)

    taken_ref[...] = jnp.zeros((1, BLK), F32)

    def attend(s_all, j):
        vt_j = vt_ref[0, j]
        key = key_ref[j]
        taken = taken_ref[...]
        tie = jnp.where(key == thr, 1.0, 0.0)
        rank = jnp.dot(tri_ref[...], tie.astype(BF16), preferred_element_type=F32)
        tie_ok = jnp.where(rank <= need - taken, tie, 0.0)
        chosen = jnp.where(key > thr, 1.0, tie_ok) > 0.5
        taken_ref[...] = taken + jnp.sum(tie, axis=0, keepdims=True)
        for h in range(N_HEADS):
            s = jnp.where(chosen, s_all[h], NEG)
            _flash_step(s, _block_offset(SLOPES_D[h], i, j), h, _head_rows(vt_j, h), m_ref, acc_ref)

    _for_blocks(i + 1, scores, attend)
    _write_heads(o_ref, [_normalised(acc_ref, h) for h in range(N_HEADS)])


def _dsa(att, vt, idx, iwt, bias, tri, seq):
    b = att.shape[0]
    nblk = seq // BLK
    topk = float(min(DSA_TOPK, seq // 4))
    return pl.pallas_call(
        functools.partial(_dsa_kernel, topk),
        out_shape=jax.ShapeDtypeStruct((b, seq, 256), BF16),
        grid=(b, nblk),
        in_specs=_attn_specs(seq, 4, 5, 2) + [
            pl.BlockSpec((1, BLK, LANE), lambda bi, i: (bi, i, 0)),
            pl.BlockSpec((1, seq, LANE), lambda bi, i: (bi, 0, 1)),
            pl.BlockSpec((1, 1, SUBLANE, BLK), lambda bi, i: (bi, i, 0, 0)),
            pl.BlockSpec((N_HEADS, BLK, BLK), lambda bi, i: (0, 0, 0)),
            pl.BlockSpec((BLK, BLK), lambda bi, i: (0, 0))],
        out_specs=pl.BlockSpec((1, BLK, 256), lambda bi, i: (bi, i, 0)),
        scratch_shapes=_state_scratch(N_HEADS) + [pltpu.VMEM((nblk, BLK, BLK), jnp.int32),
                                                  pltpu.VMEM((nblk, BLK, BLK), jnp.int16),
                                                  pltpu.VMEM((nblk, BLK, BLK), jnp.int16),
                                                  pltpu.VMEM((1, BLK), F32)],
        compiler_params=pltpu.CompilerParams(vmem_limit_bytes=VMEM_LIMIT),
        name="dsa_attn",
    )(att, att, vt, idx, idx, iwt, bias, tri)


def _pool_kernel(tp, u_ref, halo_ref, w_ref, ps_ref, o_ref, a_ref, b_ref):
    i = pl.program_id(1)
    u = u_ref[0]
    body = POOL_HALO + tp
    a_ref[0:POOL_HALO] = jnp.zeros((POOL_HALO, 256), F32)
    b_ref[0:POOL_HALO] = jnp.zeros((POOL_HALO, 256), F32)
    a_ref[POOL_HALO:2 * POOL_HALO] = jnp.where(i > 0, halo_ref[0], 0.0)
    a_ref[2 * POOL_HALO:] = u
    group = jnp.right_shift(lax.broadcasted_iota(jnp.int32, (tp, 256), 1), 6)
    src, dst = a_ref, b_ref
    win = jnp.zeros((tp, 256), F32)
    shift = 1
    for g, w in enumerate(POOL_WINDOWS):
        while shift < w:
            dst[POOL_HALO:] = src[POOL_HALO:] + src[POOL_HALO - shift:POOL_HALO - shift + body]
            src, dst = dst, src
            shift *= 2
        win = jnp.where(group == g, src[2 * POOL_HALO:], win)
    t = i * tp + lax.broadcasted_iota(jnp.int32, (tp, 256), 0)
    width = jnp.where(group == 0, 2, jnp.where(group == 1, 4, jnp.where(group == 2, 8, 16)))
    cnt = jnp.minimum(t + 1, width).astype(F32)
    d = win / cnt - u
    y = jnp.dot(d.astype(BF16), w_ref[...], preferred_element_type=F32) * ps_ref[...]
    o_ref[0] = y.astype(o_ref.dtype)


def _pool(pu, w_pool_bd, pool_scale, tp):
    b, seq, _ = pu.shape
    per = tp // POOL_HALO
    return pl.pallas_call(
        functools.partial(_pool_kernel, tp),
        out_shape=jax.ShapeDtypeStruct((b, seq, 256), BF16),
        grid=(b, seq // tp),
        in_specs=[pl.BlockSpec((1, tp, 256), lambda bi, i: (bi, i, 0)),
                  pl.BlockSpec((1, POOL_HALO, 256), lambda bi, i: (bi, jnp.maximum(i * per - 1, 0), 0)),
                  pl.BlockSpec((256, 256), lambda bi, i: (0, 0)),
                  pl.BlockSpec((1, 256), lambda bi, i: (0, 0))],
        out_specs=pl.BlockSpec((1, tp, 256), lambda bi, i: (bi, i, 0)),
        scratch_shapes=[pltpu.VMEM((tp + 2 * POOL_HALO, 256), F32),
                        pltpu.VMEM((tp + 2 * POOL_HALO, 256), F32)],
        compiler_params=pltpu.CompilerParams(vmem_limit_bytes=VMEM_LIMIT),
        name="pool_mixer",
    )(pu, pu, w_pool_bd, pool_scale)


def _merge_kernel(x_ref, mod_ref, g_ref, ya_ref, yb_ref, yc_ref, yd_ref, wg_ref, bg_ref, wb_ref, wo_ref, o_ref):
    mod = mod_ref[0]
    x = x_ref[...]
    hb = _modulated_norm(x, g_ref[...], mod[0:1], mod[1:2]).astype(BF16)
    merged = jnp.zeros(x.shape, F32)
    for n, y_ref in enumerate((ya_ref, yb_ref, yc_ref, yd_ref)):
        gate = jax.nn.sigmoid(jnp.dot(hb, wg_ref[n], preferred_element_type=F32) + bg_ref[n])
        merged = merged + gate * jnp.dot(y_ref[...], wb_ref[n], preferred_element_type=F32)
    o_ref[...] = x + mod[2:3] * jnp.dot(merged.astype(BF16), wo_ref[...], preferred_element_type=F32)


def _resident(shape):
    zeros = (0,) * len(shape)
    return pl.BlockSpec(shape, lambda i: zeros, pipeline_mode=pl.Buffered(1))


def _merge(x2, mod, g, ys, w_gate, b_gate, w_branch, w_out, seq, tm):
    t, d = x2.shape
    tok = lambda w: pl.BlockSpec((tm, w), lambda i: (i, 0))
    return pl.pallas_call(
        _merge_kernel,
        out_shape=jax.ShapeDtypeStruct((t, d), F32),
        grid=(t // tm,),
        in_specs=[tok(d),
                  pl.BlockSpec((1, 6, d), lambda i: (i * tm // seq, 0, 0)),
                  _resident((1, d)),
                  tok(256), tok(256), tok(256), tok(256),
                  _resident(w_gate.shape), _resident(b_gate.shape), _resident(w_branch.shape),
                  _resident(w_out.shape)],
        out_specs=tok(d),
        compiler_params=pltpu.CompilerParams(vmem_limit_bytes=VMEM_LIMIT),
        name="gated_merge",
    )(x2, mod, g, *ys, w_gate, b_gate, w_branch, w_out)


def _ffn_kernel(final, x_ref, mod_ref, g_ref, w1_ref, w2_ref, fg_ref, o_ref):
    mod = mod_ref[0]
    x = x_ref[...]
    hb = _modulated_norm(x, g_ref[...], mod[3:4], mod[4:5]).astype(BF16)
    d = x.shape[1]
    ff = jnp.zeros(x.shape, F32)
    for c in range(w1_ref.shape[1] // d):
        a = jnp.maximum(jnp.dot(hb, w1_ref[:, c * d:(c + 1) * d], preferred_element_type=F32), 0.0)
        ff = ff + jnp.dot((a * a).astype(BF16), w2_ref[c * d:(c + 1) * d, :], preferred_element_type=F32)
    out = x + mod[5:6] * ff
    if final:
        ms = jnp.mean(out * out, axis=-1, keepdims=True)
        out = out * lax.rsqrt(ms + NORM_EPS) * fg_ref[...]
    o_ref[...] = out


def _ffn(x2, mod, g, w1, w2, final_g, final, seq, tm):
    t, d = x2.shape
    tok = pl.BlockSpec((tm, d), lambda i: (i, 0))
    return pl.pallas_call(
        functools.partial(_ffn_kernel, final),
        out_shape=jax.ShapeDtypeStruct((t, d), F32),
        grid=(t // tm,),
        in_specs=[tok,
                  pl.BlockSpec((1, 6, d), lambda i: (i * tm // seq, 0, 0)),
                  _resident((1, d)), _resident(w1.shape), _resident(w2.shape), _resident((1, d))],
        out_specs=tok,
        compiler_params=pltpu.CompilerParams(vmem_limit_bytes=VMEM_LIMIT),
        name="ffn",
    )(x2, mod, g, w1, w2, final_g)


def _alibi_tables(slopes):
    out = np.zeros((len(slopes), BLK, 256), np.float32)
    for h, m in enumerate(slopes):
        rest = (m * LOG2E * np.arange(BLK, dtype=np.float64)).astype(np.float32)
        lo = _bias_lane(h // (len(slopes) // 2))
        for term in range(N_BIAS_TERMS):
            piece = rest.astype(BF16).astype(np.float32)
            out[h, :, lo + term] = piece
            rest = rest - piece
    return jnp.asarray(out.astype(BF16))


def _reorder_w_in(w_in):
    o = np.cumsum([0, 256, 256, 256, 256, 256, 256, 256, 256, 256, 256, 128, 32, 4])
    piece = lambda n: w_in[:, o[n]:o[n + 1]]
    d = w_in.shape[0]
    cols = [piece(n) for n in (0, 1, 3, 4, 7, 8, 2, 5, 9, 6, 10)]
    cols += [piece(11)] * (LANE // IDX_DH)
    cols += [piece(12), jnp.zeros((d, LANE - N_IDX_HEADS), w_in.dtype)]
    return jnp.concatenate(cols, axis=1)


def _q_colscale():
    s = np.ones((1, ATT_W), np.float32)
    s[0, 0:256] = HEAD ** -0.5 * LOG2E
    s[0, 512:768] = DIFF_DH ** -0.5 * LOG2E
    s[0, 1024:1280] = HEAD ** -0.5 * LOG2E
    return jnp.asarray(s)


def _block_diag(w_pool):
    g, c, _ = w_pool.shape
    out = jnp.zeros((g * c, g * c), w_pool.dtype)
    for n in range(g):
        out = out.at[n * c:(n + 1) * c, n * c:(n + 1) * c].set(w_pool[n])
    return out


def kernel(x, c, w_ada, b_ada, norm1_g, w_in, diff_lambda, diff_subln_g, w_pool, pool_scale, w_gate, b_gate,
           w_branch, w_out, norm2_g, w_ff1, w_ff2, final_norm_g):
    b, seq, d = x.shape
    depth = w_ada.shape[0]
    t = b * seq
    tm = 512
    nblk = seq // BLK
    nb_pad = -(-nblk // SUBLANE) * SUBLANE
    assert seq % tm == 0
    mods = _modulation(c, w_ada, b_ada).reshape(depth, b, 6, d)
    bias_a, bias_b, bias_d = _alibi_tables(SLOPES_A), _alibi_tables(SLOPES_B), _alibi_tables(SLOPES_D)
    tri = jnp.asarray(np.tril(np.ones((BLK, BLK), np.float32))).astype(BF16)
    colscale = _q_colscale()
    x2 = x.reshape(t, d)
    for l in range(depth):
        mod = mods[l]
        g1 = norm1_g[l].reshape(1, d)
        att, vt, pu, idx, iwt, kmean = _in_proj(x2, mod, g1, _reorder_w_in(w_in[l]).astype(BF16), colscale,
                                                b, seq, tm)
        att = att.reshape(b, seq, ATT_W)
        kmean_pad = jnp.pad(kmean.reshape(b, nblk, 256), ((0, 0), (0, nb_pad - nblk), (0, 0)))
        y_a = _moba(att, vt, kmean_pad, bias_a, seq)
        lambda_init = 0.8 - 0.6 * math.exp(-0.3 * l)
        y_b = _diff(att, vt, bias_b, diff_lambda[l], jnp.tile(diff_subln_g[l], 2).reshape(1, LANE), lambda_init, seq)
        y_c = _pool(pu.reshape(b, seq, 256), _block_diag(w_pool[l]).astype(BF16), pool_scale[l].reshape(1, 256), tm)
        y_d = _dsa(att, vt, idx.reshape(b, seq, 2 * LANE), iwt, bias_d, tri, seq)
        ys = [y.reshape(t, 256) for y in (y_a, y_b, y_c, y_d)]
        x2 = _merge(x2, mod, g1, ys, w_gate[l].astype(BF16), b_gate[l][:, None, :], w_branch[l].astype(BF16),
                    w_out[l].astype(BF16), seq, tm)
        x2 = _ffn(x2, mod, norm2_g[l].reshape(1, d), w_ff1[l].astype(BF16), w_ff2[l].astype(BF16),
                  final_norm_g.reshape(1, d), l == depth - 1, seq, tm)
    return x2.reshape(b, seq, d)
```

```python
import functools
import math

import numpy as np
import jax
import jax.numpy as jnp
from jax import lax
from jax.experimental import pallas as pl
from jax.experimental.pallas import tpu as pltpu

F32 = jnp.float32
BF16 = jnp.bfloat16
HIGHEST = lax.Precision.HIGHEST

NEG = -1e30
BELOW_NEG = -3.0e38
NORM_EPS = 1e-6
LOG2E = 1.4426950408889634
BLK = 256
LANE = 128
SUBLANE = 8
HEAD = 64
N_HEADS = 4
VT_ROWS = HEAD + 16
DIFF_DH = 32
MOBA_TOPK = 3
DSA_TOPK = 256
IDX_DH = 32
N_IDX_HEADS = 4
POOL_WINDOWS = (2, 4, 8, 16)
POOL_HALO = 16
INT_MIN = -(2 ** 31)
HALF_BITS = 16
HALF_MASK = 2 ** HALF_BITS - 1
INT16_MIN = -(2 ** (HALF_BITS - 1))
PACKED_SUBLANE = 16
VMEM_LIMIT = 56 * 1024 * 1024

_ALL_SLOPES = [float(np.float32(2.0 ** (-8.0 * i / 12))) for i in range(1, 13)]
SLOPES_A = _ALL_SLOPES[0::3]
SLOPES_B = _ALL_SLOPES[1::3]
SLOPES_D = _ALL_SLOPES[2::3]

ATT_W = 6 * 256
V_OFF = ATT_W
V_W = 3 * 256
PU_OFF = V_OFF + V_W
IDX_OFF = PU_OFF + 256
IN_W = IDX_OFF + 3 * LANE


def _key_of_neg():
    bits = int(np.array(NEG, np.float32).view(np.int32))
    return INT_MIN - bits


KEY_NEG = _key_of_neg()


def _nt(a, b, precision=None):
    return lax.dot_general(a, b, (((1,), (1,)), ((), ())), preferred_element_type=F32, precision=precision)


def _modulated_norm(x, g, shift, scale):
    ms = jnp.mean(x * x, axis=-1, keepdims=True)
    y = x * lax.rsqrt(ms + NORM_EPS) * g
    return y * (1.0 + scale) + shift


def _mod_kernel(c_ref, w_ref, b_ref, o_ref):
    c = c_ref[...]
    act = c * jax.nn.sigmoid(c)
    o_ref[0] = jnp.dot(act, w_ref[0], preferred_element_type=F32, precision=HIGHEST) + b_ref[0]


def _modulation(c, w_ada, b_ada):
    depth, d, n = w_ada.shape
    b = c.shape[0]
    tn = n // 4
    return pl.pallas_call(
        _mod_kernel,
        out_shape=jax.ShapeDtypeStruct((depth, b, n), F32),
        grid=(depth, n // tn),
        in_specs=[pl.BlockSpec((b, d), lambda l, j: (0, 0)),
                  pl.BlockSpec((1, d, tn), lambda l, j: (l, 0, j)),
                  pl.BlockSpec((1, 1, tn), lambda l, j: (l, 0, j))],
        out_specs=pl.BlockSpec((1, b, tn), lambda l, j: (l, 0, j)),
        compiler_params=pltpu.CompilerParams(vmem_limit_bytes=VMEM_LIMIT),
        name="adaln_mod",
    )(c, w_ada, b_ada.reshape(depth, 1, n))


def _in_kernel(x_ref, mod_ref, g_ref, w_ref, cs_ref, att_ref, vt_ref, pu_ref, idx_ref, iwt_ref, km_ref):
    mod = mod_ref[0]
    h = _modulated_norm(x_ref[...], g_ref[...], mod[0:1], mod[1:2])
    proj = jnp.dot(h.astype(BF16), w_ref[...], preferred_element_type=F32)
    att_ref[...] = (proj[:, :ATT_W] * cs_ref[...]).astype(BF16)
    pu_ref[...] = proj[:, PU_OFF:PU_OFF + 256]
    idx_ref[...] = proj[:, IDX_OFF:IDX_OFF + 2 * LANE].astype(BF16)
    for r in range(km_ref.shape[0]):
        rows = slice(r * BLK, (r + 1) * BLK)
        km_ref[r] = jnp.mean(proj[rows, 256:512], axis=0, keepdims=True)
        for g in range(3):
            vt = proj[rows, V_OFF + g * 256:V_OFF + (g + 1) * 256].T.astype(BF16)
            for hd in range(N_HEADS):
                base = (g * N_HEADS + hd) * VT_ROWS
                vt_ref[0, r, base:base + HEAD, :] = vt[hd * HEAD:(hd + 1) * HEAD]
                vt_ref[0, r, base + HEAD:base + VT_ROWS, :] = jnp.ones((VT_ROWS - HEAD, BLK), BF16)
        iwt_ref[0, r] = proj[rows, IDX_OFF + 2 * LANE:].T[0:SUBLANE, :]


def _in_proj(x2, mod, g, w_in_r, colscale, b, seq, tm):
    t, d = x2.shape
    nb = tm // BLK
    per_seq = seq // tm
    vt_rows = 3 * N_HEADS * VT_ROWS
    return pl.pallas_call(
        _in_kernel,
        out_shape=(jax.ShapeDtypeStruct((t, ATT_W), BF16),
                   jax.ShapeDtypeStruct((b, seq // BLK, vt_rows, BLK), BF16),
                   jax.ShapeDtypeStruct((t, 256), F32),
                   jax.ShapeDtypeStruct((t, 2 * LANE), BF16),
                   jax.ShapeDtypeStruct((b, seq // BLK, SUBLANE, BLK), F32),
                   jax.ShapeDtypeStruct((t // BLK, 1, 256), F32)),
        grid=(t // tm,),
        in_specs=[pl.BlockSpec((tm, d), lambda i: (i, 0)),
                  pl.BlockSpec((1, 6, d), lambda i: (i // per_seq, 0, 0)),
                  pl.BlockSpec((1, d), lambda i: (0, 0)),
                  pl.BlockSpec((d, IN_W), lambda i: (0, 0)),
                  pl.BlockSpec((1, ATT_W), lambda i: (0, 0))],
        out_specs=(pl.BlockSpec((tm, ATT_W), lambda i: (i, 0)),
                   pl.BlockSpec((1, nb, vt_rows, BLK), lambda i: (i // per_seq, i % per_seq, 0, 0)),
                   pl.BlockSpec((tm, 256), lambda i: (i, 0)),
                   pl.BlockSpec((tm, 2 * LANE), lambda i: (i, 0)),
                   pl.BlockSpec((1, nb, SUBLANE, BLK), lambda i: (i // per_seq, i % per_seq, 0, 0)),
                   pl.BlockSpec((nb, 1, 256), lambda i: (i, 0, 0))),
        compiler_params=pltpu.CompilerParams(vmem_limit_bytes=VMEM_LIMIT),
        name="in_proj",
    )(x2, mod, g, w_in_r, colscale)


N_BIAS_TERMS = 3


def _feature_masks(width):
    lane = lax.broadcasted_iota(jnp.int32, (BLK, 256), 1)
    return [(lane >= lo) & (lane < lo + width) for lo in range(0, 256, width)]


def _bias_lane(group):
    return LANE if group == 0 else 0


def _group_queries(q):
    lane = lax.broadcasted_iota(jnp.int32, (BLK, 256), 1)
    out = []
    for group in range(2):
        lo = _bias_lane(group)
        ones = jnp.where((lane >= lo) & (lane < lo + N_BIAS_TERMS), 1.0, 0.0).astype(q.dtype)
        out.append(jnp.where((lane < LANE) == (group == 0), q, ones))
    return out


def _stacked_scores(k_blk, masks, fills, q_groups):
    half = len(masks) // 2
    tiles = []
    for group in range(2):
        rows = []
        for n in range(group * half, (group + 1) * half):
            fill = jnp.zeros((), k_blk.dtype) if fills is None else fills[n]
            rows.append(jnp.where(masks[n], k_blk, fill))
        s = _nt(jnp.concatenate(rows, axis=0), q_groups[group])
        tiles += [s[n * BLK:(n + 1) * BLK] for n in range(half)]
    return tiles


def _causal_t():
    sl = lax.broadcasted_iota(jnp.int32, (BLK, BLK), 0)
    tl = lax.broadcasted_iota(jnp.int32, (BLK, BLK), 1)
    return tl - sl


def _init_state(m_ref, acc_ref):
    m_ref[...] = jnp.full(m_ref.shape, NEG, F32)
    acc_ref[...] = jnp.zeros(acc_ref.shape, F32)


def _flash_step(s, c, n, vt_h, m_ref, acc_ref, chosen=None):
    s3 = s.reshape(BLK // SUBLANE, SUBLANE, BLK)
    m_blk = jnp.max(s3, axis=0)
    for step in (4, 2, 1):
        m_blk = jnp.maximum(m_blk, pltpu.roll(m_blk, step, 0))
    m_old = m_ref[n]
    m_new = jnp.maximum(m_old, m_blk + c)
    shift = m_new - c
    if chosen is not None:
        m_new = jnp.where(chosen, m_new, m_old)
        shift = jnp.where(chosen, shift, -NEG)
    p = jnp.exp2(s3 - shift[None]).reshape(BLK, BLK)
    alpha = jnp.exp2(m_old - m_new)
    pv = jnp.dot(vt_h, p.astype(BF16), preferred_element_type=F32)
    groups = VT_ROWS // SUBLANE
    acc = acc_ref[n].reshape(groups, SUBLANE, BLK) * alpha[None] + pv.reshape(groups, SUBLANE, BLK)
    acc_ref[n] = acc.reshape(VT_ROWS, BLK)
    m_ref[n] = m_new


def _block_offset(slope, i, j):
    steps = jnp.full((1, 1), i - j, jnp.int32).astype(F32)
    return steps * (-slope * LOG2E * BLK)


class _Held:
    def __init__(self, ref):
        self.ref = ref
        self.vals = [ref[n] for n in range(ref.shape[0])]

    def __getitem__(self, n):
        return self.vals[n]

    def __setitem__(self, n, v):
        self.vals[n] = v

    def flush(self):
        for n, v in enumerate(self.vals):
            self.ref[n] = v


def _for_blocks(n, scores, consume, m_ref, acc_ref, consume_last=None):
    def run(first, count, last=None):
        tiles = [scores(first + u) for u in range(count)]
        m, acc = _Held(m_ref), _Held(acc_ref)
        for u in range(count):
            (last if last is not None and u == count - 1 else consume)(tiles[u], first + u, m, acc)
        m.flush()
        acc.flush()

    def trip(t, carry):
        run(4 * t, 4)
        return carry

    held = 0 if consume_last is None else jnp.minimum(n, 2)
    m = n - held
    lax.fori_loop(0, m // 4, trip, 0)
    rest = m % 4

    @pl.when(rest >= 2)
    def _():
        run(m - rest, 2)

    @pl.when(rest % 2 == 1)
    def _():
        run(m - 1, 1)

    if consume_last is not None:
        @pl.when(n >= 2)
        def _():
            run(n - 2, 2, consume_last)

        @pl.when(n == 1)
        def _():
            run(0, 1, consume_last)


def _normalised(acc_ref, n):
    acc = acc_ref[n]
    return acc[0:HEAD] / acc[HEAD:HEAD + 1]


def _write_heads(o_ref, outs):
    for pair in range(2):
        both = jnp.concatenate([outs[2 * pair], outs[2 * pair + 1]], axis=0)
        o_ref[0, :, pair * LANE:(pair + 1) * LANE] = both.T.astype(o_ref.dtype)


def _attn_specs(seq, q_col, k_col, v_group):
    nblk = seq // BLK
    return [pl.BlockSpec((1, BLK, 256), lambda bi, i: (bi, i, q_col)),
            pl.BlockSpec((1, seq, 256), lambda bi, i: (bi, 0, k_col)),
            pl.BlockSpec((1, nblk, N_HEADS * VT_ROWS, BLK), lambda bi, i: (bi, 0, v_group, 0))]


def _state_scratch(n):
    return [pltpu.VMEM((n, SUBLANE, BLK), F32), pltpu.VMEM((n, VT_ROWS, BLK), F32)]


def _head_rows(vt_blk, h):
    return vt_blk[h * VT_ROWS:(h + 1) * VT_ROWS, :]


def _moba_kernel(q_ref, k_ref, vt_ref, km_ref, fill_ref, o_ref, m_ref, acc_ref, sel_ref):
    i = pl.program_id(1)
    nb = km_ref.shape[1]
    masks = _feature_masks(HEAD)
    row = lax.broadcasted_iota(jnp.int32, (nb, BLK), 0)
    row_f = row.astype(F32)
    causal = _causal_t() >= 0
    _init_state(m_ref, acc_ref)
    q = q_ref[0]
    q_groups = _group_queries(q)

    km = km_ref[0]
    km_head = lax.broadcasted_iota(jnp.int32, (nb, 256), 1) // HEAD
    terms = []
    rest = jnp.concatenate([jnp.where(km_head == h, km, 0.0) for h in range(N_HEADS)], axis=0)
    for _ in range(3):
        part = rest.astype(BF16)
        terms.append(part)
        rest = rest - part.astype(F32)
    gates = _nt(jnp.concatenate(terms, axis=0), q)
    rows = N_HEADS * nb
    gates = gates[0:rows] + gates[rows:2 * rows] + gates[2 * rows:3 * rows]
    for h in range(N_HEADS):
        g = jnp.where(row < i, gates[h * nb:(h + 1) * nb], NEG)
        sel = jnp.zeros((nb, BLK), F32)
        for _ in range(MOBA_TOPK):
            mx = jnp.max(g, axis=0, keepdims=True)
            first = jnp.min(jnp.where(g == mx, row_f, 1e9), axis=0, keepdims=True)
            pick = row_f == first
            sel = jnp.where(pick, 1.0, sel)
            g = jnp.where(pick, BELOW_NEG, g)
        sel_ref[h] = jnp.where(row < i, sel, 0.0)

    def scores(j):
        start = pl.multiple_of(j * BLK, BLK)
        fills = [fill_ref[h] for h in range(N_HEADS)]
        return _stacked_scores(k_ref[0, pl.ds(start, BLK), :], masks, fills, q_groups)

    def consume(s_all, j, m, acc):
        vt_j = vt_ref[0, j]
        for h in range(N_HEADS):
            chosen = sel_ref[h, pl.ds(j, 1), :] > 0.5
            _flash_step(s_all[h], _block_offset(SLOPES_A[h], i, j), h, _head_rows(vt_j, h), m, acc, chosen)

    def consume_own(s_own, j, m, acc):
        vt_own = vt_ref[0, j]
        for h in range(N_HEADS):
            _flash_step(jnp.where(causal, s_own[h], NEG), 0.0, h, _head_rows(vt_own, h), m, acc)

    _for_blocks(i + 1, scores, consume, m_ref, acc_ref, consume_own)
    _write_heads(o_ref, [_normalised(acc_ref, h) for h in range(N_HEADS)])


def _moba(att, vt, kmean_pad, bias, seq):
    b = att.shape[0]
    nb = kmean_pad.shape[1]
    return pl.pallas_call(
        _moba_kernel,
        out_shape=jax.ShapeDtypeStruct((b, seq, 256), BF16),
        grid=(b, seq // BLK),
        in_specs=_attn_specs(seq, 0, 1, 0) + [
            pl.BlockSpec((1, nb, 256), lambda bi, i: (bi, 0, 0)),
            pl.BlockSpec((N_HEADS, BLK, BLK), lambda bi, i: (0, 0, 0))],
        out_specs=pl.BlockSpec((1, BLK, 256), lambda bi, i: (bi, i, 0)),
        scratch_shapes=_state_scratch(N_HEADS) + [pltpu.VMEM((N_HEADS, nb, BLK), F32)],
        compiler_params=pltpu.CompilerParams(vmem_limit_bytes=VMEM_LIMIT),
        name="moba_attn",
    )(att, att, vt, kmean_pad, bias)


def _diff_kernel(lambda_init, q_ref, k_ref, vt_ref, fill_ref, dl_ref, g_ref, o_ref, m_ref, acc_ref):
    i = pl.program_id(1)
    masks = _feature_masks(DIFF_DH)
    causal = _causal_t() >= 0
    _init_state(m_ref, acc_ref)
    q_groups = _group_queries(q_ref[0])

    def scores(j):
        start = pl.multiple_of(j * BLK, BLK)
        fills = [fill_ref[n // 2] for n in range(2 * N_HEADS)]
        return _stacked_scores(k_ref[0, pl.ds(start, BLK), :], masks, fills, q_groups)

    def consume(s_all, j, m, acc, own):
        vt_j = vt_ref[0, j]
        for n in range(2 * N_HEADS):
            h = n // 2
            s = jnp.where(causal, s_all[n], NEG) if own else s_all[n]
            c = 0.0 if own else _block_offset(SLOPES_B[h], i, j)
            _flash_step(s, c, n, _head_rows(vt_j, h), m, acc)

    _for_blocks(i + 1, scores, functools.partial(consume, own=False), m_ref, acc_ref,
                functools.partial(consume, own=True))

    dl = dl_ref[...]
    lam = (jnp.exp(jnp.sum(dl[0:1] * dl[1:2], axis=1, keepdims=True))
           - jnp.exp(jnp.sum(dl[2:3] * dl[3:4], axis=1, keepdims=True)) + lambda_init)
    lane = lax.broadcasted_iota(jnp.int32, (BLK, LANE), 1)
    for pair in range(2):
        both = []
        for h in (2 * pair, 2 * pair + 1):
            both.append(_normalised(acc_ref, 2 * h) - lam * _normalised(acc_ref, 2 * h + 1))
        o = jnp.concatenate(both, axis=0).T
        sq = o * o
        ms = jnp.where(lane < HEAD,
                       jnp.sum(jnp.where(lane < HEAD, sq, 0.0), axis=1, keepdims=True),
                       jnp.sum(jnp.where(lane >= HEAD, sq, 0.0), axis=1, keepdims=True)) * (1.0 / HEAD)
        y = o * lax.rsqrt(ms + NORM_EPS) * g_ref[...] * (1.0 - lambda_init)
        o_ref[0, :, pair * LANE:(pair + 1) * LANE] = y.astype(o_ref.dtype)


def _diff(att, vt, bias, diff_lambda, subln_g2, lambda_init, seq):
    b = att.shape[0]
    return pl.pallas_call(
        functools.partial(_diff_kernel, lambda_init),
        out_shape=jax.ShapeDtypeStruct((b, seq, 256), BF16),
        grid=(b, seq // BLK),
        in_specs=_attn_specs(seq, 2, 3, 1) + [
            pl.BlockSpec((N_HEADS, BLK, BLK), lambda bi, i: (0, 0, 0)),
            pl.BlockSpec((4, DIFF_DH), lambda bi, i: (0, 0)),
            pl.BlockSpec((1, LANE), lambda bi, i: (0, 0))],
        out_specs=pl.BlockSpec((1, BLK, 256), lambda bi, i: (bi, i, 0)),
        scratch_shapes=_state_scratch(2 * N_HEADS),
        compiler_params=pltpu.CompilerParams(vmem_limit_bytes=VMEM_LIMIT),
        name="diff_attn",
    )(att, att, vt, bias, diff_lambda, subln_g2)


def _dsa_kernel(topk, q_ref, k_ref, vt_ref, iq_ref, ik_ref, iwt_ref, fill_ref, tri_ref, o_ref,
                m_ref, acc_ref, key_ref, hi_ref, lo_ref, taken_ref):
    i = pl.program_id(1)
    dist = _causal_t()
    _init_state(m_ref, acc_ref)

    iq = iq_ref[0]
    lane = lax.broadcasted_iota(jnp.int32, (BLK, LANE), 1)
    idx_masks = [(lane >= h * IDX_DH) & (lane < (h + 1) * IDX_DH) for h in range(N_IDX_HEADS)]
    iwt = iwt_ref[0, 0]
    wrow = [iwt[h:h + 1, :] * (N_IDX_HEADS ** -0.5 * IDX_DH ** -0.5) for h in range(N_IDX_HEADS)]

    def score_block(j, own):
        start = pl.multiple_of(j * BLK, BLK)
        rel_all = _stacked_scores(ik_ref[0, pl.ds(start, BLK), :], idx_masks, None, (iq, iq))
        sc = jnp.maximum(rel_all[0], 0.0) * wrow[0]
        for h in range(1, N_IDX_HEADS):
            sc = sc + jnp.maximum(rel_all[h], 0.0) * wrow[h]
        if own:
            sc = jnp.where(dist >= 0, sc, NEG)
        bits = lax.bitcast_convert_type(sc, jnp.int32)
        key = jnp.where(bits < 0, INT_MIN - bits, bits)
        key_ref[j] = key
        hi_ref[j] = lax.shift_right_arithmetic(key, HALF_BITS).astype(jnp.int16)
        lo_ref[j] = (jnp.bitwise_and(key, HALF_MASK) + INT16_MIN).astype(jnp.int16)

    def past_pair(t, carry):
        score_block(2 * t, False)
        score_block(2 * t + 1, False)
        return carry

    lax.fori_loop(0, i // 2, past_pair, 0)

    @pl.when(i % 2 == 1)
    def _():
        score_block(i - 1, False)

    score_block(i, True)

    def count_where(ref, pred):
        rows = BLK // PACKED_SUBLANE

        def blk(j, acc):
            hit = jnp.where(pred(ref[j]), jnp.int16(1), jnp.int16(0))
            groups = hit.reshape(rows, PACKED_SUBLANE, BLK)
            parts = [groups[r] for r in range(rows)]
            while len(parts) > 1:
                parts = [a + b for a, b in zip(parts[0::2], parts[1::2])]
            return acc + parts[0].astype(jnp.int32)

        def pair(t, acc):
            return blk(2 * t + 1, blk(2 * t, acc))

        acc = lax.fori_loop(0, (i + 1) // 2, pair, jnp.zeros((PACKED_SUBLANE, BLK), jnp.int32))
        acc = lax.cond(i % 2 == 0, lambda a: blk(i, a), lambda a: a, acc)
        return jnp.sum(acc, axis=0, keepdims=True).astype(F32)

    def half_search(ref, base):
        def step(n, ans):
            cand = ans + jnp.left_shift(jnp.int32(1), HALF_BITS - 1 - n)
            cand16 = cand.astype(jnp.int16)
            cnt = base + count_where(ref, lambda half: half >= cand16)
            return jnp.where(cnt >= topk, cand, ans)

        return lax.fori_loop(0, HALF_BITS, step, jnp.full((1, BLK), INT16_MIN, jnp.int32))

    thr_hi = half_search(hi_ref, 0.0)
    thr_hi16 = thr_hi.astype(jnp.int16)
    above_hi = count_where(hi_ref, lambda half: half > thr_hi16)

    def keep_bucket(j, carry):
        lo_ref[j] = jnp.where(hi_ref[j] == thr_hi16, lo_ref[j], jnp.int16(INT16_MIN))
        return carry

    lax.fori_loop(0, i + 1, keep_bucket, 0)
    thr_lo = half_search(lo_ref, above_hi)
    thr_lo16 = thr_lo.astype(jnp.int16)
    n_above = above_hi + count_where(lo_ref, lambda half: half > thr_lo16)
    thr = jnp.left_shift(thr_hi, HALF_BITS) + (thr_lo - INT16_MIN)
    need = jnp.where(thr == KEY_NEG, 0.0, topk - n_above)

    q_groups = _group_queries(q_ref[0])
    masks = _feature_masks(HEAD)

    def scores(j):
        start = pl.multiple_of(j * BLK, BLK)
        fills = [fill_ref[h] for h in range(N_HEADS)]
        return _stacked_scores(k_ref[0, pl.ds(start, BLK), :], masks, fills, q_groups)

    taken_ref[...] = jnp.zeros((1, BLK), F32)

    def attend(s_all, j, m, acc):
        vt_j = vt_ref[0, j]
        key = key_ref[j]
        taken = taken_ref[...]
        tie = jnp.where(key == thr, 1.0, 0.0)
        rank = jnp.dot(tri_ref[...], tie.astype(BF16), preferred_element_type=F32)
        tie_ok = jnp.where(rank <= need - taken, tie, 0.0)
        chosen = jnp.where(key > thr, 1.0, tie_ok) > 0.5
        taken_ref[...] = taken + jnp.sum(tie, axis=0, keepdims=True)
        for h in range(N_HEADS):
            s = jnp.where(chosen, s_all[h], NEG)
            _flash_step(s, _block_offset(SLOPES_D[h], i, j), h, _head_rows(vt_j, h), m, acc)

    _for_blocks(i + 1, scores, attend, m_ref, acc_ref)
    _write_heads(o_ref, [_normalised(acc_ref, h) for h in range(N_HEADS)])


def _dsa(att, vt, idx, iwt, bias, tri, seq):
    b = att.shape[0]
    nblk = seq // BLK
    topk = float(min(DSA_TOPK, seq // 4))
    return pl.pallas_call(
        functools.partial(_dsa_kernel, topk),
        out_shape=jax.ShapeDtypeStruct((b, seq, 256), BF16),
        grid=(b, nblk),
        in_specs=_attn_specs(seq, 4, 5, 2) + [
            pl.BlockSpec((1, BLK, LANE), lambda bi, i: (bi, i, 0)),
            pl.BlockSpec((1, seq, LANE), lambda bi, i: (bi, 0, 1)),
            pl.BlockSpec((1, 1, SUBLANE, BLK), lambda bi, i: (bi, i, 0, 0)),
            pl.BlockSpec((N_HEADS, BLK, BLK), lambda bi, i: (0, 0, 0)),
            pl.BlockSpec((BLK, BLK), lambda bi, i: (0, 0))],
        out_specs=pl.BlockSpec((1, BLK, 256), lambda bi, i: (bi, i, 0)),
        scratch_shapes=_state_scratch(N_HEADS) + [pltpu.VMEM((nblk, BLK, BLK), jnp.int32),
                                                  pltpu.VMEM((nblk, BLK, BLK), jnp.int16),
                                                  pltpu.VMEM((nblk, BLK, BLK), jnp.int16),
                                                  pltpu.VMEM((1, BLK), F32)],
        compiler_params=pltpu.CompilerParams(vmem_limit_bytes=VMEM_LIMIT),
        name="dsa_attn",
    )(att, att, vt, idx, idx, iwt, bias, tri)


def _pool_kernel(tp, u_ref, halo_ref, w_ref, ps_ref, o_ref, a_ref, b_ref):
    i = pl.program_id(1)
    u = u_ref[0]
    body = POOL_HALO + tp
    a_ref[0:POOL_HALO] = jnp.zeros((POOL_HALO, 256), F32)
    b_ref[0:POOL_HALO] = jnp.zeros((POOL_HALO, 256), F32)
    a_ref[POOL_HALO:2 * POOL_HALO] = jnp.where(i > 0, halo_ref[0], 0.0)
    a_ref[2 * POOL_HALO:] = u
    group = jnp.right_shift(lax.broadcasted_iota(jnp.int32, (tp, 256), 1), 6)
    src, dst = a_ref, b_ref
    win = jnp.zeros((tp, 256), F32)
    shift = 1
    for g, w in enumerate(POOL_WINDOWS):
        while shift < w:
            dst[POOL_HALO:] = src[POOL_HALO:] + src[POOL_HALO - shift:POOL_HALO - shift + body]
            src, dst = dst, src
            shift *= 2
        win = jnp.where(group == g, src[2 * POOL_HALO:], win)
    t = i * tp + lax.broadcasted_iota(jnp.int32, (tp, 256), 0)
    width = jnp.where(group == 0, 2, jnp.where(group == 1, 4, jnp.where(group == 2, 8, 16)))
    cnt = jnp.minimum(t + 1, width).astype(F32)
    d = win / cnt - u
    y = jnp.dot(d.astype(BF16), w_ref[...], preferred_element_type=F32) * ps_ref[...]
    o_ref[0] = y.astype(o_ref.dtype)


def _pool(pu, w_pool_bd, pool_scale, tp):
    b, seq, _ = pu.shape
    per = tp // POOL_HALO
    return pl.pallas_call(
        functools.partial(_pool_kernel, tp),
        out_shape=jax.ShapeDtypeStruct((b, seq, 256), BF16),
        grid=(b, seq // tp),
        in_specs=[pl.BlockSpec((1, tp, 256), lambda bi, i: (bi, i, 0)),
                  pl.BlockSpec((1, POOL_HALO, 256), lambda bi, i: (bi, jnp.maximum(i * per - 1, 0), 0)),
                  pl.BlockSpec((256, 256), lambda bi, i: (0, 0)),
                  pl.BlockSpec((1, 256), lambda bi, i: (0, 0))],
        out_specs=pl.BlockSpec((1, tp, 256), lambda bi, i: (bi, i, 0)),
        scratch_shapes=[pltpu.VMEM((tp + 2 * POOL_HALO, 256), F32),
                        pltpu.VMEM((tp + 2 * POOL_HALO, 256), F32)],
        compiler_params=pltpu.CompilerParams(vmem_limit_bytes=VMEM_LIMIT),
        name="pool_mixer",
    )(pu, pu, w_pool_bd, pool_scale)


def _merge_kernel(x_ref, mod_ref, g_ref, ya_ref, yb_ref, yc_ref, yd_ref, wg_ref, bg_ref, wb_ref, wo_ref, o_ref):
    mod = mod_ref[0]
    x = x_ref[...]
    hb = _modulated_norm(x, g_ref[...], mod[0:1], mod[1:2]).astype(BF16)
    merged = jnp.zeros(x.shape, F32)
    for n, y_ref in enumerate((ya_ref, yb_ref, yc_ref, yd_ref)):
        gate = jax.nn.sigmoid(jnp.dot(hb, wg_ref[n], preferred_element_type=F32) + bg_ref[n])
        merged = merged + gate * jnp.dot(y_ref[...], wb_ref[n], preferred_element_type=F32)
    o_ref[...] = x + mod[2:3] * jnp.dot(merged.astype(BF16), wo_ref[...], preferred_element_type=F32)


def _resident(shape):
    zeros = (0,) * len(shape)
    return pl.BlockSpec(shape, lambda i: zeros, pipeline_mode=pl.Buffered(1))


def _merge(x2, mod, g, ys, w_gate, b_gate, w_branch, w_out, seq, tm):
    t, d = x2.shape
    tok = lambda w: pl.BlockSpec((tm, w), lambda i: (i, 0))
    return pl.pallas_call(
        _merge_kernel,
        out_shape=jax.ShapeDtypeStruct((t, d), F32),
        grid=(t // tm,),
        in_specs=[tok(d),
                  pl.BlockSpec((1, 6, d), lambda i: (i * tm // seq, 0, 0)),
                  _resident((1, d)),
                  tok(256), tok(256), tok(256), tok(256),
                  _resident(w_gate.shape), _resident(b_gate.shape), _resident(w_branch.shape),
                  _resident(w_out.shape)],
        out_specs=tok(d),
        compiler_params=pltpu.CompilerParams(vmem_limit_bytes=VMEM_LIMIT),
        name="gated_merge",
    )(x2, mod, g, *ys, w_gate, b_gate, w_branch, w_out)


def _ffn_kernel(final, x_ref, mod_ref, g_ref, w1_ref, w2_ref, fg_ref, o_ref):
    mod = mod_ref[0]
    x = x_ref[...]
    hb = _modulated_norm(x, g_ref[...], mod[3:4], mod[4:5]).astype(BF16)
    d = x.shape[1]
    ff = jnp.zeros(x.shape, F32)
    for c in range(w1_ref.shape[1] // d):
        a = jnp.maximum(jnp.dot(hb, w1_ref[:, c * d:(c + 1) * d], preferred_element_type=F32), 0.0)
        ff = ff + jnp.dot((a * a).astype(BF16), w2_ref[c * d:(c + 1) * d, :], preferred_element_type=F32)
    out = x + mod[5:6] * ff
    if final:
        ms = jnp.mean(out * out, axis=-1, keepdims=True)
        out = out * lax.rsqrt(ms + NORM_EPS) * fg_ref[...]
    o_ref[...] = out


def _ffn(x2, mod, g, w1, w2, final_g, final, seq, tm):
    t, d = x2.shape
    tok = pl.BlockSpec((tm, d), lambda i: (i, 0))
    return pl.pallas_call(
        functools.partial(_ffn_kernel, final),
        out_shape=jax.ShapeDtypeStruct((t, d), F32),
        grid=(t // tm,),
        in_specs=[tok,
                  pl.BlockSpec((1, 6, d), lambda i: (i * tm // seq, 0, 0)),
                  _resident((1, d)), _resident(w1.shape), _resident(w2.shape), _resident((1, d))],
        out_specs=tok,
        compiler_params=pltpu.CompilerParams(vmem_limit_bytes=VMEM_LIMIT),
        name="ffn",
    )(x2, mod, g, w1, w2, final_g)


def _alibi_tables(slopes):
    out = np.zeros((len(slopes), BLK, 256), np.float32)
    for h, m in enumerate(slopes):
        rest = (m * LOG2E * np.arange(BLK, dtype=np.float64)).astype(np.float32)
        lo = _bias_lane(h // (len(slopes) // 2))
        for term in range(N_BIAS_TERMS):
            piece = rest.astype(BF16).astype(np.float32)
            out[h, :, lo + term] = piece
            rest = rest - piece
    return jnp.asarray(out.astype(BF16))


def _reorder_w_in(w_in):
    o = np.cumsum([0, 256, 256, 256, 256, 256, 256, 256, 256, 256, 256, 128, 32, 4])
    piece = lambda n: w_in[:, o[n]:o[n + 1]]
    d = w_in.shape[0]
    cols = [piece(n) for n in (0, 1, 3, 4, 7, 8, 2, 5, 9, 6, 10)]
    cols += [piece(11)] * (LANE // IDX_DH)
    cols += [piece(12), jnp.zeros((d, LANE - N_IDX_HEADS), w_in.dtype)]
    return jnp.concatenate(cols, axis=1)


def _q_colscale():
    s = np.ones((1, ATT_W), np.float32)
    s[0, 0:256] = HEAD ** -0.5 * LOG2E
    s[0, 512:768] = DIFF_DH ** -0.5 * LOG2E
    s[0, 1024:1280] = HEAD ** -0.5 * LOG2E
    return jnp.asarray(s)


def _block_diag(w_pool):
    g, c, _ = w_pool.shape
    out = jnp.zeros((g * c, g * c), w_pool.dtype)
    for n in range(g):
        out = out.at[n * c:(n + 1) * c, n * c:(n + 1) * c].set(w_pool[n])
    return out


def kernel(x, c, w_ada, b_ada, norm1_g, w_in, diff_lambda, diff_subln_g, w_pool, pool_scale, w_gate, b_gate,
           w_branch, w_out, norm2_g, w_ff1, w_ff2, final_norm_g):
    b, seq, d = x.shape
    depth = w_ada.shape[0]
    t = b * seq
    tm = 512
    nblk = seq // BLK
    nb_pad = -(-nblk // SUBLANE) * SUBLANE
    assert seq % tm == 0
    mods = _modulation(c, w_ada, b_ada).reshape(depth, b, 6, d)
    bias_a, bias_b, bias_d = _alibi_tables(SLOPES_A), _alibi_tables(SLOPES_B), _alibi_tables(SLOPES_D)
    tri = jnp.asarray(np.tril(np.ones((BLK, BLK), np.float32))).astype(BF16)
    colscale = _q_colscale()
    x2 = x.reshape(t, d)
    for l in range(depth):
        mod = mods[l]
        g1 = norm1_g[l].reshape(1, d)
        att, vt, pu, idx, iwt, kmean = _in_proj(x2, mod, g1, _reorder_w_in(w_in[l]).astype(BF16), colscale,
                                                b, seq, tm)
        att = att.reshape(b, seq, ATT_W)
        kmean_pad = jnp.pad(kmean.reshape(b, nblk, 256), ((0, 0), (0, nb_pad - nblk), (0, 0)))
        y_a = _moba(att, vt, kmean_pad, bias_a, seq)
        lambda_init = 0.8 - 0.6 * math.exp(-0.3 * l)
        y_b = _diff(att, vt, bias_b, diff_lambda[l], jnp.tile(diff_subln_g[l], 2).reshape(1, LANE), lambda_init, seq)
        y_c = _pool(pu.reshape(b, seq, 256), _block_diag(w_pool[l]).astype(BF16), pool_scale[l].reshape(1, 256), tm)
        y_d = _dsa(att, vt, idx.reshape(b, seq, 2 * LANE), iwt, bias_d, tri, seq)
        ys = [y.reshape(t, 256) for y in (y_a, y_b, y_c, y_d)]
        x2 = _merge(x2, mod, g1, ys, w_gate[l].astype(BF16), b_gate[l][:, None, :], w_branch[l].astype(BF16),
                    w_out[l].astype(BF16), seq, tm)
        x2 = _ffn(x2, mod, norm2_g[l].reshape(1, d), w_ff1[l].astype(BF16), w_ff2[l].astype(BF16),
                  final_norm_g.reshape(1, d), l == depth - 1, seq, tm)
    return x2.reshape(b, seq, d)
```

```python
import functools
import math

import numpy as np
import jax
import jax.numpy as jnp
from jax import lax
from jax.experimental import pallas as pl
from jax.experimental.pallas import tpu as pltpu

F32 = jnp.float32
BF16 = jnp.bfloat16
HIGHEST = lax.Precision.HIGHEST

NEG = -1e30
BELOW_NEG = -3.0e38
NORM_EPS = 1e-6
LOG2E = 1.4426950408889634
BLK = 256
LANE = 128
SUBLANE = 8
HEAD = 64
N_HEADS = 4
VT_ROWS = HEAD + 16
DIFF_DH = 32
MOBA_TOPK = 3
DSA_TOPK = 256
IDX_DH = 32
N_IDX_HEADS = 4
POOL_WINDOWS = (2, 4, 8, 16)
POOL_HALO = 16
INT_MIN = -(2 ** 31)
HALF_BITS = 16
HALF_MASK = 2 ** HALF_BITS - 1
INT16_MIN = -(2 ** (HALF_BITS - 1))
PACKED_SUBLANE = 16
VMEM_LIMIT = 56 * 1024 * 1024

_ALL_SLOPES = [float(np.float32(2.0 ** (-8.0 * i / 12))) for i in range(1, 13)]
SLOPES_A = _ALL_SLOPES[0::3]
SLOPES_B = _ALL_SLOPES[1::3]
SLOPES_D = _ALL_SLOPES[2::3]

ATT_W = 6 * 256
V_OFF = ATT_W
V_W = 3 * 256
PU_OFF = V_OFF + V_W
IDX_OFF = PU_OFF + 256
IN_W = IDX_OFF + 3 * LANE


def _key_of_neg():
    bits = int(np.array(NEG, np.float32).view(np.int32))
    return INT_MIN - bits


KEY_NEG = _key_of_neg()


def _nt(a, b, precision=None):
    return lax.dot_general(a, b, (((1,), (1,)), ((), ())), preferred_element_type=F32, precision=precision)


def _modulated_norm(x, g, shift, scale):
    ms = jnp.mean(x * x, axis=-1, keepdims=True)
    y = x * lax.rsqrt(ms + NORM_EPS) * g
    return y * (1.0 + scale) + shift


def _mod_kernel(c_ref, w_ref, b_ref, o_ref):
    c = c_ref[...]
    act = c * jax.nn.sigmoid(c)
    o_ref[0] = jnp.dot(act, w_ref[0], preferred_element_type=F32, precision=HIGHEST) + b_ref[0]


def _modulation(c, w_ada, b_ada):
    depth, d, n = w_ada.shape
    b = c.shape[0]
    tn = n // 4
    return pl.pallas_call(
        _mod_kernel,
        out_shape=jax.ShapeDtypeStruct((depth, b, n), F32),
        grid=(depth, n // tn),
        in_specs=[pl.BlockSpec((b, d), lambda l, j: (0, 0)),
                  pl.BlockSpec((1, d, tn), lambda l, j: (l, 0, j)),
                  pl.BlockSpec((1, 1, tn), lambda l, j: (l, 0, j))],
        out_specs=pl.BlockSpec((1, b, tn), lambda l, j: (l, 0, j)),
        compiler_params=pltpu.CompilerParams(vmem_limit_bytes=VMEM_LIMIT),
        name="adaln_mod",
    )(c, w_ada, b_ada.reshape(depth, 1, n))


def _in_kernel(x_ref, mod_ref, g_ref, w_ref, cs_ref, att_ref, vt_ref, pu_ref, idx_ref, iwt_ref, km_ref):
    mod = mod_ref[0]
    h = _modulated_norm(x_ref[...], g_ref[...], mod[0:1], mod[1:2])
    proj = jnp.dot(h.astype(BF16), w_ref[...], preferred_element_type=F32)
    att_ref[...] = (proj[:, :ATT_W] * cs_ref[...]).astype(BF16)
    pu_ref[...] = proj[:, PU_OFF:PU_OFF + 256]
    idx_ref[...] = proj[:, IDX_OFF:IDX_OFF + 2 * LANE].astype(BF16)
    for r in range(km_ref.shape[0]):
        rows = slice(r * BLK, (r + 1) * BLK)
        km_ref[r] = jnp.mean(proj[rows, 256:512], axis=0, keepdims=True)
        for g in range(3):
            vt = proj[rows, V_OFF + g * 256:V_OFF + (g + 1) * 256].T.astype(BF16)
            for hd in range(N_HEADS):
                base = (g * N_HEADS + hd) * VT_ROWS
                vt_ref[0, r, base:base + HEAD, :] = vt[hd * HEAD:(hd + 1) * HEAD]
                vt_ref[0, r, base + HEAD:base + VT_ROWS, :] = jnp.ones((VT_ROWS - HEAD, BLK), BF16)
        iwt_ref[0, r] = proj[rows, IDX_OFF + 2 * LANE:].T[0:SUBLANE, :]


def _in_proj(x2, mod, g, w_in_r, colscale, b, seq, tm):
    t, d = x2.shape
    nb = tm // BLK
    per_seq = seq // tm
    vt_rows = 3 * N_HEADS * VT_ROWS
    return pl.pallas_call(
        _in_kernel,
        out_shape=(jax.ShapeDtypeStruct((t, ATT_W), BF16),
                   jax.ShapeDtypeStruct((b, seq // BLK, vt_rows, BLK), BF16),
                   jax.ShapeDtypeStruct((t, 256), F32),
                   jax.ShapeDtypeStruct((t, 2 * LANE), BF16),
                   jax.ShapeDtypeStruct((b, seq // BLK, SUBLANE, BLK), F32),
                   jax.ShapeDtypeStruct((t // BLK, 1, 256), F32)),
        grid=(t // tm,),
        in_specs=[pl.BlockSpec((tm, d), lambda i: (i, 0)),
                  pl.BlockSpec((1, 6, d), lambda i: (i // per_seq, 0, 0)),
                  pl.BlockSpec((1, d), lambda i: (0, 0)),
                  pl.BlockSpec((d, IN_W), lambda i: (0, 0)),
                  pl.BlockSpec((1, ATT_W), lambda i: (0, 0))],
        out_specs=(pl.BlockSpec((tm, ATT_W), lambda i: (i, 0)),
                   pl.BlockSpec((1, nb, vt_rows, BLK), lambda i: (i // per_seq, i % per_seq, 0, 0)),
                   pl.BlockSpec((tm, 256), lambda i: (i, 0)),
                   pl.BlockSpec((tm, 2 * LANE), lambda i: (i, 0)),
                   pl.BlockSpec((1, nb, SUBLANE, BLK), lambda i: (i // per_seq, i % per_seq, 0, 0)),
                   pl.BlockSpec((nb, 1, 256), lambda i: (i, 0, 0))),
        compiler_params=pltpu.CompilerParams(vmem_limit_bytes=VMEM_LIMIT),
        name="in_proj",
    )(x2, mod, g, w_in_r, colscale)


N_BIAS_TERMS = 3


def _feature_masks(width):
    lane = lax.broadcasted_iota(jnp.int32, (BLK, 256), 1)
    return [(lane >= lo) & (lane < lo + width) for lo in range(0, 256, width)]


def _bias_lane(group):
    return LANE if group == 0 else 0


def _group_queries(q):
    lane = lax.broadcasted_iota(jnp.int32, (BLK, 256), 1)
    out = []
    for group in range(2):
        lo = _bias_lane(group)
        ones = jnp.where((lane >= lo) & (lane < lo + N_BIAS_TERMS), 1.0, 0.0).astype(q.dtype)
        out.append(jnp.where((lane < LANE) == (group == 0), q, ones))
    return out


def _stacked_scores(k_blk, masks, fills, q_groups):
    half = len(masks) // 2
    tiles = []
    for group in range(2):
        rows = []
        for n in range(group * half, (group + 1) * half):
            fill = jnp.zeros((), k_blk.dtype) if fills is None else fills[n]
            rows.append(jnp.where(masks[n], k_blk, fill))
        s = _nt(jnp.concatenate(rows, axis=0), q_groups[group])
        tiles += [s[n * BLK:(n + 1) * BLK] for n in range(half)]
    return tiles


def _causal_t():
    sl = lax.broadcasted_iota(jnp.int32, (BLK, BLK), 0)
    tl = lax.broadcasted_iota(jnp.int32, (BLK, BLK), 1)
    return tl - sl


def _init_state(m_ref, acc_ref):
    m_ref[...] = jnp.full(m_ref.shape, NEG, F32)
    acc_ref[...] = jnp.zeros(acc_ref.shape, F32)


def _flash_step(s, c, n, vt_h, m_ref, acc_ref, chosen=None):
    s3 = s.reshape(BLK // SUBLANE, SUBLANE, BLK)
    m_blk = jnp.max(s3, axis=0)
    for step in (4, 2, 1):
        m_blk = jnp.maximum(m_blk, pltpu.roll(m_blk, step, 0))
    m_old = m_ref[n]
    m_new = jnp.maximum(m_old, m_blk + c)
    shift = m_new - c
    if chosen is not None:
        m_new = jnp.where(chosen, m_new, m_old)
        shift = jnp.where(chosen, shift, -NEG)
    p = jnp.exp2(s3 - shift[None]).reshape(BLK, BLK)
    alpha = jnp.exp2(m_old - m_new)
    pv = jnp.dot(vt_h, p.astype(BF16), preferred_element_type=F32)
    groups = VT_ROWS // SUBLANE
    acc = acc_ref[n].reshape(groups, SUBLANE, BLK) * alpha[None] + pv.reshape(groups, SUBLANE, BLK)
    acc_ref[n] = acc.reshape(VT_ROWS, BLK)
    m_ref[n] = m_new


def _block_offset(slope, i, j):
    steps = jnp.full((1, 1), i - j, jnp.int32).astype(F32)
    return steps * (-slope * LOG2E * BLK)


class _Held:
    def __init__(self, ref):
        self.ref = ref
        self.vals = [ref[n] for n in range(ref.shape[0])]

    def __getitem__(self, n):
        return self.vals[n]

    def __setitem__(self, n, v):
        self.vals[n] = v

    def flush(self):
        for n, v in enumerate(self.vals):
            self.ref[n] = v


def _for_blocks(n, scores, consume, m_ref, acc_ref, consume_last=None):
    def run(first, count, last=None):
        tiles = [scores(first + u) for u in range(count)]
        m, acc = _Held(m_ref), _Held(acc_ref)
        for u in range(count):
            (last if last is not None and u == count - 1 else consume)(tiles[u], first + u, m, acc)
        m.flush()
        acc.flush()

    def trip(t, carry):
        run(4 * t, 4)
        return carry

    held = 0 if consume_last is None else jnp.minimum(n, 2)
    m = n - held
    lax.fori_loop(0, m // 4, trip, 0)
    rest = m % 4

    @pl.when(rest >= 2)
    def _():
        run(m - rest, 2)

    @pl.when(rest % 2 == 1)
    def _():
        run(m - 1, 1)

    if consume_last is not None:
        @pl.when(n >= 2)
        def _():
            run(n - 2, 2, consume_last)

        @pl.when(n == 1)
        def _():
            run(0, 1, consume_last)


def _normalised(acc_ref, n):
    acc = acc_ref[n]
    return acc[0:HEAD] / acc[HEAD:HEAD + 1]


def _write_heads(o_ref, outs):
    for pair in range(2):
        both = jnp.concatenate([outs[2 * pair], outs[2 * pair + 1]], axis=0)
        o_ref[0, :, pair * LANE:(pair + 1) * LANE] = both.T.astype(o_ref.dtype)


def _attn_specs(seq, q_col, k_col, v_group):
    nblk = seq // BLK
    return [pl.BlockSpec((1, BLK, 256), lambda bi, i: (bi, i, q_col)),
            pl.BlockSpec((1, seq, 256), lambda bi, i: (bi, 0, k_col)),
            pl.BlockSpec((1, nblk, N_HEADS * VT_ROWS, BLK), lambda bi, i: (bi, 0, v_group, 0))]


def _state_scratch(n):
    return [pltpu.VMEM((n, SUBLANE, BLK), F32), pltpu.VMEM((n, VT_ROWS, BLK), F32)]


def _head_rows(vt_blk, h):
    return vt_blk[h * VT_ROWS:(h + 1) * VT_ROWS, :]


def _moba_kernel(q_ref, k_ref, vt_ref, km_ref, fill_ref, o_ref, m_ref, acc_ref, sel_ref):
    i = pl.program_id(1)
    nb = km_ref.shape[1]
    masks = _feature_masks(HEAD)
    row = lax.broadcasted_iota(jnp.int32, (nb, BLK), 0)
    row_f = row.astype(F32)
    causal = _causal_t() >= 0
    _init_state(m_ref, acc_ref)
    q = q_ref[0]
    q_groups = _group_queries(q)

    km = km_ref[0]
    km_head = lax.broadcasted_iota(jnp.int32, (nb, 256), 1) // HEAD
    terms = []
    rest = jnp.concatenate([jnp.where(km_head == h, km, 0.0) for h in range(N_HEADS)], axis=0)
    for _ in range(3):
        part = rest.astype(BF16)
        terms.append(part)
        rest = rest - part.astype(F32)
    gates = _nt(jnp.concatenate(terms, axis=0), q)
    rows = N_HEADS * nb
    gates = gates[0:rows] + gates[rows:2 * rows] + gates[2 * rows:3 * rows]
    for h in range(N_HEADS):
        g = jnp.where(row < i, gates[h * nb:(h + 1) * nb], NEG)
        sel = jnp.zeros((nb, BLK), F32)
        for _ in range(MOBA_TOPK):
            mx = jnp.max(g, axis=0, keepdims=True)
            first = jnp.min(jnp.where(g == mx, row_f, 1e9), axis=0, keepdims=True)
            pick = row_f == first
            sel = jnp.where(pick, 1.0, sel)
            g = jnp.where(pick, BELOW_NEG, g)
        sel_ref[h] = jnp.where(row < i, sel, 0.0)

    def scores(j):
        start = pl.multiple_of(j * BLK, BLK)
        fills = [fill_ref[h] for h in range(N_HEADS)]
        return _stacked_scores(k_ref[0, pl.ds(start, BLK), :], masks, fills, q_groups)

    def consume(s_all, j, m, acc):
        vt_j = vt_ref[0, j]
        for h in range(N_HEADS):
            chosen = sel_ref[h, pl.ds(j, 1), :] > 0.5
            _flash_step(s_all[h], _block_offset(SLOPES_A[h], i, j), h, _head_rows(vt_j, h), m, acc, chosen)

    def consume_own(s_own, j, m, acc):
        vt_own = vt_ref[0, j]
        for h in range(N_HEADS):
            _flash_step(jnp.where(causal, s_own[h], NEG), 0.0, h, _head_rows(vt_own, h), m, acc)

    _for_blocks(i + 1, scores, consume, m_ref, acc_ref, consume_own)
    _write_heads(o_ref, [_normalised(acc_ref, h) for h in range(N_HEADS)])


def _moba(att, vt, kmean_pad, bias, seq):
    b = att.shape[0]
    nb = kmean_pad.shape[1]
    return pl.pallas_call(
        _moba_kernel,
        out_shape=jax.ShapeDtypeStruct((b, seq, 256), BF16),
        grid=(b, seq // BLK),
        in_specs=_attn_specs(seq, 0, 1, 0) + [
            pl.BlockSpec((1, nb, 256), lambda bi, i: (bi, 0, 0)),
            pl.BlockSpec((N_HEADS, BLK, BLK), lambda bi, i: (0, 0, 0))],
        out_specs=pl.BlockSpec((1, BLK, 256), lambda bi, i: (bi, i, 0)),
        scratch_shapes=_state_scratch(N_HEADS) + [pltpu.VMEM((N_HEADS, nb, BLK), F32)],
        compiler_params=pltpu.CompilerParams(vmem_limit_bytes=VMEM_LIMIT),
        name="moba_attn",
    )(att, att, vt, kmean_pad, bias)


def _diff_kernel(lambda_init, q_ref, k_ref, vt_ref, fill_ref, dl_ref, g_ref, o_ref, m_ref, acc_ref):
    i = pl.program_id(1)
    masks = _feature_masks(DIFF_DH)
    causal = _causal_t() >= 0
    _init_state(m_ref, acc_ref)
    q_groups = _group_queries(q_ref[0])

    def scores(j):
        start = pl.multiple_of(j * BLK, BLK)
        fills = [fill_ref[n // 2] for n in range(2 * N_HEADS)]
        return _stacked_scores(k_ref[0, pl.ds(start, BLK), :], masks, fills, q_groups)

    def consume(s_all, j, m, acc, own):
        vt_j = vt_ref[0, j]
        for n in range(2 * N_HEADS):
            h = n // 2
            s = jnp.where(causal, s_all[n], NEG) if own else s_all[n]
            c = 0.0 if own else _block_offset(SLOPES_B[h], i, j)
            _flash_step(s, c, n, _head_rows(vt_j, h), m, acc)

    _for_blocks(i + 1, scores, functools.partial(consume, own=False), m_ref, acc_ref,
                functools.partial(consume, own=True))

    dl = dl_ref[...]
    lam = (jnp.exp(jnp.sum(dl[0:1] * dl[1:2], axis=1, keepdims=True))
           - jnp.exp(jnp.sum(dl[2:3] * dl[3:4], axis=1, keepdims=True)) + lambda_init)
    lane = lax.broadcasted_iota(jnp.int32, (BLK, LANE), 1)
    for pair in range(2):
        both = []
        for h in (2 * pair, 2 * pair + 1):
            both.append(_normalised(acc_ref, 2 * h) - lam * _normalised(acc_ref, 2 * h + 1))
        o = jnp.concatenate(both, axis=0).T
        sq = o * o
        ms = jnp.where(lane < HEAD,
                       jnp.sum(jnp.where(lane < HEAD, sq, 0.0), axis=1, keepdims=True),
                       jnp.sum(jnp.where(lane >= HEAD, sq, 0.0), axis=1, keepdims=True)) * (1.0 / HEAD)
        y = o * lax.rsqrt(ms + NORM_EPS) * g_ref[...] * (1.0 - lambda_init)
        o_ref[0, :, pair * LANE:(pair + 1) * LANE] = y.astype(o_ref.dtype)


def _diff(att, vt, bias, diff_lambda, subln_g2, lambda_init, seq):
    b = att.shape[0]
    return pl.pallas_call(
        functools.partial(_diff_kernel, lambda_init),
        out_shape=jax.ShapeDtypeStruct((b, seq, 256), BF16),
        grid=(b, seq // BLK),
        in_specs=_attn_specs(seq, 2, 3, 1) + [
            pl.BlockSpec((N_HEADS, BLK, BLK), lambda bi, i: (0, 0, 0)),
            pl.BlockSpec((4, DIFF_DH), lambda bi, i: (0, 0)),
            pl.BlockSpec((1, LANE), lambda bi, i: (0, 0))],
        out_specs=pl.BlockSpec((1, BLK, 256), lambda bi, i: (bi, i, 0)),
        scratch_shapes=_state_scratch(2 * N_HEADS),
        compiler_params=pltpu.CompilerParams(vmem_limit_bytes=VMEM_LIMIT),
        name="diff_attn",
    )(att, att, vt, bias, diff_lambda, subln_g2)


def _dsa_kernel(topk, q_ref, k_ref, vt_ref, iq_ref, ik_ref, iwt_ref, fill_ref, tri_ref, o_ref,
                m_ref, acc_ref, key_ref, hi_ref, lo_ref):
    i = pl.program_id(1)
    dist = _causal_t()
    _init_state(m_ref, acc_ref)

    iq = iq_ref[0]
    lane = lax.broadcasted_iota(jnp.int32, (BLK, LANE), 1)
    idx_masks = [(lane >= h * IDX_DH) & (lane < (h + 1) * IDX_DH) for h in range(N_IDX_HEADS)]
    iwt = iwt_ref[0, 0]
    wrow = [iwt[h:h + 1, :] * (N_IDX_HEADS ** -0.5 * IDX_DH ** -0.5) for h in range(N_IDX_HEADS)]

    def score_block(j, own):
        start = pl.multiple_of(j * BLK, BLK)
        rel_all = _stacked_scores(ik_ref[0, pl.ds(start, BLK), :], idx_masks, None, (iq, iq))
        sc = jnp.maximum(rel_all[0], 0.0) * wrow[0]
        for h in range(1, N_IDX_HEADS):
            sc = sc + jnp.maximum(rel_all[h], 0.0) * wrow[h]
        if own:
            sc = jnp.where(dist >= 0, sc, NEG)
        bits = lax.bitcast_convert_type(sc, jnp.int32)
        key = jnp.where(bits < 0, INT_MIN - bits, bits)
        key_ref[j] = key
        hi_ref[j] = lax.shift_right_arithmetic(key, HALF_BITS).astype(jnp.int16)
        lo_ref[j] = (jnp.bitwise_and(key, HALF_MASK) + INT16_MIN).astype(jnp.int16)

    def past_pair(t, carry):
        score_block(2 * t, False)
        score_block(2 * t + 1, False)
        return carry

    lax.fori_loop(0, i // 2, past_pair, 0)

    @pl.when(i % 2 == 1)
    def _():
        score_block(i - 1, False)

    score_block(i, True)

    def count_where(ref, pred):
        rows = BLK // PACKED_SUBLANE

        def blk(j, acc):
            hit = jnp.where(pred(ref[j]), jnp.int16(1), jnp.int16(0))
            groups = hit.reshape(rows, PACKED_SUBLANE, BLK)
            parts = [groups[r] for r in range(rows)]
            while len(parts) > 1:
                parts = [a + b for a, b in zip(parts[0::2], parts[1::2])]
            return acc + parts[0].astype(jnp.int32)

        def pair(t, acc):
            return blk(2 * t + 1, blk(2 * t, acc))

        acc = lax.fori_loop(0, (i + 1) // 2, pair, jnp.zeros((PACKED_SUBLANE, BLK), jnp.int32))
        acc = lax.cond(i % 2 == 0, lambda a: blk(i, a), lambda a: a, acc)
        return jnp.sum(acc, axis=0, keepdims=True).astype(F32)

    def half_search(ref, base):
        def step(n, ans):
            cand = ans + jnp.left_shift(jnp.int32(1), HALF_BITS - 1 - n)
            cand16 = cand.astype(jnp.int16)
            cnt = base + count_where(ref, lambda half: half >= cand16)
            return jnp.where(cnt >= topk, cand, ans)

        return lax.fori_loop(0, HALF_BITS, step, jnp.full((1, BLK), INT16_MIN, jnp.int32))

    thr_hi = half_search(hi_ref, 0.0)
    thr_hi16 = thr_hi.astype(jnp.int16)
    above_hi = count_where(hi_ref, lambda half: half > thr_hi16)

    def keep_bucket(j, carry):
        lo_ref[j] = jnp.where(hi_ref[j] == thr_hi16, lo_ref[j], jnp.int16(INT16_MIN))
        return carry

    lax.fori_loop(0, i + 1, keep_bucket, 0)
    thr_lo = half_search(lo_ref, above_hi)
    thr_lo16 = thr_lo.astype(jnp.int16)
    n_above = above_hi + count_where(lo_ref, lambda half: half > thr_lo16)
    thr = jnp.left_shift(thr_hi, HALF_BITS) + (thr_lo - INT16_MIN)
    need = jnp.where(thr == KEY_NEG, 0.0, topk - n_above)

    def tie_rank(j):
        key = key_ref[j]
        tied = key == thr
        tie = jnp.where(tied, 1.0, 0.0)
        return key, tied, tie, jnp.dot(tri_ref[...], tie.astype(BF16), preferred_element_type=F32)

    def mark(j, parts, taken):
        key, tied, tie, rank = parts
        tie_bias = jnp.where(rank <= need - taken, 0.0, NEG)
        bias = jnp.where(key > thr, 0.0, jnp.where(tied, tie_bias, NEG))
        key_ref[j] = lax.bitcast_convert_type(bias, jnp.int32)
        return taken + jnp.sum(tie, axis=0, keepdims=True)

    def mark_pair(t, taken):
        first, second = tie_rank(2 * t), tie_rank(2 * t + 1)
        return mark(2 * t + 1, second, mark(2 * t, first, taken))

    taken = lax.fori_loop(0, (i + 1) // 2, mark_pair, jnp.zeros((1, BLK), F32))
    lax.cond(i % 2 == 0, lambda tk: mark(i, tie_rank(i), tk), lambda tk: tk, taken)

    q_groups = _group_queries(q_ref[0])
    masks = _feature_masks(HEAD)

    def scores(j):
        start = pl.multiple_of(j * BLK, BLK)
        fills = [fill_ref[h] for h in range(N_HEADS)]
        return _stacked_scores(k_ref[0, pl.ds(start, BLK), :], masks, fills, q_groups)

    def attend(s_all, j, m, acc):
        vt_j = vt_ref[0, j]
        bias = lax.bitcast_convert_type(key_ref[j], F32)
        for h in range(N_HEADS):
            _flash_step(s_all[h] + bias, _block_offset(SLOPES_D[h], i, j), h, _head_rows(vt_j, h), m, acc)

    _for_blocks(i + 1, scores, attend, m_ref, acc_ref)
    _write_heads(o_ref, [_normalised(acc_ref, h) for h in range(N_HEADS)])


def _dsa(att, vt, idx, iwt, bias, tri, seq):
    b = att.shape[0]
    nblk = seq // BLK
    topk = float(min(DSA_TOPK, seq // 4))
    return pl.pallas_call(
        functools.partial(_dsa_kernel, topk),
        out_shape=jax.ShapeDtypeStruct((b, seq, 256), BF16),
        grid=(b, nblk),
        in_specs=_attn_specs(seq, 4, 5, 2) + [
            pl.BlockSpec((1, BLK, LANE), lambda bi, i: (bi, i, 0)),
            pl.BlockSpec((1, seq, LANE), lambda bi, i: (bi, 0, 1)),
            pl.BlockSpec((1, 1, SUBLANE, BLK), lambda bi, i: (bi, i, 0, 0)),
            pl.BlockSpec((N_HEADS, BLK, BLK), lambda bi, i: (0, 0, 0)),
            pl.BlockSpec((BLK, BLK), lambda bi, i: (0, 0))],
        out_specs=pl.BlockSpec((1, BLK, 256), lambda bi, i: (bi, i, 0)),
        scratch_shapes=_state_scratch(N_HEADS) + [pltpu.VMEM((nblk, BLK, BLK), jnp.int32),
                                                  pltpu.VMEM((nblk, BLK, BLK), jnp.int16),
                                                  pltpu.VMEM((nblk, BLK, BLK), jnp.int16)],
        compiler_params=pltpu.CompilerParams(vmem_limit_bytes=VMEM_LIMIT),
        name="dsa_attn",
    )(att, att, vt, idx, idx, iwt, bias, tri)


def _pool_kernel(tp, u_ref, halo_ref, w_ref, ps_ref, o_ref, a_ref, b_ref):
    i = pl.program_id(1)
    u = u_ref[0]
    body = POOL_HALO + tp
    a_ref[0:POOL_HALO] = jnp.zeros((POOL_HALO, 256), F32)
    b_ref[0:POOL_HALO] = jnp.zeros((POOL_HALO, 256), F32)
    a_ref[POOL_HALO:2 * POOL_HALO] = jnp.where(i > 0, halo_ref[0], 0.0)
    a_ref[2 * POOL_HALO:] = u
    group = jnp.right_shift(lax.broadcasted_iota(jnp.int32, (tp, 256), 1), 6)
    src, dst = a_ref, b_ref
    win = jnp.zeros((tp, 256), F32)
    shift = 1
    for g, w in enumerate(POOL_WINDOWS):
        while shift < w:
            dst[POOL_HALO:] = src[POOL_HALO:] + src[POOL_HALO - shift:POOL_HALO - shift + body]
            src, dst = dst, src
            shift *= 2
        win = jnp.where(group == g, src[2 * POOL_HALO:], win)
    t = i * tp + lax.broadcasted_iota(jnp.int32, (tp, 256), 0)
    width = jnp.where(group == 0, 2, jnp.where(group == 1, 4, jnp.where(group == 2, 8, 16)))
    cnt = jnp.minimum(t + 1, width).astype(F32)
    d = win / cnt - u
    y = jnp.dot(d.astype(BF16), w_ref[...], preferred_element_type=F32) * ps_ref[...]
    o_ref[0] = y.astype(o_ref.dtype)


def _pool(pu, w_pool_bd, pool_scale, tp):
    b, seq, _ = pu.shape
    per = tp // POOL_HALO
    return pl.pallas_call(
        functools.partial(_pool_kernel, tp),
        out_shape=jax.ShapeDtypeStruct((b, seq, 256), BF16),
        grid=(b, seq // tp),
        in_specs=[pl.BlockSpec((1, tp, 256), lambda bi, i: (bi, i, 0)),
                  pl.BlockSpec((1, POOL_HALO, 256), lambda bi, i: (bi, jnp.maximum(i * per - 1, 0), 0)),
                  pl.BlockSpec((256, 256), lambda bi, i: (0, 0)),
                  pl.BlockSpec((1, 256), lambda bi, i: (0, 0))],
        out_specs=pl.BlockSpec((1, tp, 256), lambda bi, i: (bi, i, 0)),
        scratch_shapes=[pltpu.VMEM((tp + 2 * POOL_HALO, 256), F32),
                        pltpu.VMEM((tp + 2 * POOL_HALO, 256), F32)],
        compiler_params=pltpu.CompilerParams(vmem_limit_bytes=VMEM_LIMIT),
        name="pool_mixer",
    )(pu, pu, w_pool_bd, pool_scale)


def _merge_kernel(x_ref, mod_ref, g_ref, ya_ref, yb_ref, yc_ref, yd_ref, wg_ref, bg_ref, wb_ref, wo_ref, o_ref):
    mod = mod_ref[0]
    x = x_ref[...]
    hb = _modulated_norm(x, g_ref[...], mod[0:1], mod[1:2]).astype(BF16)
    merged = jnp.zeros(x.shape, F32)
    for n, y_ref in enumerate((ya_ref, yb_ref, yc_ref, yd_ref)):
        gate = jax.nn.sigmoid(jnp.dot(hb, wg_ref[n], preferred_element_type=F32) + bg_ref[n])
        merged = merged + gate * jnp.dot(y_ref[...], wb_ref[n], preferred_element_type=F32)
    o_ref[...] = x + mod[2:3] * jnp.dot(merged.astype(BF16), wo_ref[...], preferred_element_type=F32)


def _resident(shape):
    zeros = (0,) * len(shape)
    return pl.BlockSpec(shape, lambda i: zeros, pipeline_mode=pl.Buffered(1))


def _merge(x2, mod, g, ys, w_gate, b_gate, w_branch, w_out, seq, tm):
    t, d = x2.shape
    tok = lambda w: pl.BlockSpec((tm, w), lambda i: (i, 0))
    return pl.pallas_call(
        _merge_kernel,
        out_shape=jax.ShapeDtypeStruct((t, d), F32),
        grid=(t // tm,),
        in_specs=[tok(d),
                  pl.BlockSpec((1, 6, d), lambda i: (i * tm // seq, 0, 0)),
                  _resident((1, d)),
                  tok(256), tok(256), tok(256), tok(256),
                  _resident(w_gate.shape), _resident(b_gate.shape), _resident(w_branch.shape),
                  _resident(w_out.shape)],
        out_specs=tok(d),
        compiler_params=pltpu.CompilerParams(vmem_limit_bytes=VMEM_LIMIT),
        name="gated_merge",
    )(x2, mod, g, *ys, w_gate, b_gate, w_branch, w_out)


def _ffn_kernel(final, x_ref, mod_ref, g_ref, w1_ref, w2_ref, fg_ref, o_ref):
    mod = mod_ref[0]
    x = x_ref[...]
    hb = _modulated_norm(x, g_ref[...], mod[3:4], mod[4:5]).astype(BF16)
    d = x.shape[1]
    ff = jnp.zeros(x.shape, F32)
    for c in range(w1_ref.shape[1] // d):
        a = jnp.maximum(jnp.dot(hb, w1_ref[:, c * d:(c + 1) * d], preferred_element_type=F32), 0.0)
        ff = ff + jnp.dot((a * a).astype(BF16), w2_ref[c * d:(c + 1) * d, :], preferred_element_type=F32)
    out = x + mod[5:6] * ff
    if final:
        ms = jnp.mean(out * out, axis=-1, keepdims=True)
        out = out * lax.rsqrt(ms + NORM_EPS) * fg_ref[...]
    o_ref[...] = out


def _ffn(x2, mod, g, w1, w2, final_g, final, seq, tm):
    t, d = x2.shape
    tok = pl.BlockSpec((tm, d), lambda i: (i, 0))
    return pl.pallas_call(
        functools.partial(_ffn_kernel, final),
        out_shape=jax.ShapeDtypeStruct((t, d), F32),
        grid=(t // tm,),
        in_specs=[tok,
                  pl.BlockSpec((1, 6, d), lambda i: (i * tm // seq, 0, 0)),
                  _resident((1, d)), _resident(w1.shape), _resident(w2.shape), _resident((1, d))],
        out_specs=tok,
        compiler_params=pltpu.CompilerParams(vmem_limit_bytes=VMEM_LIMIT),
        name="ffn",
    )(x2, mod, g, w1, w2, final_g)


def _alibi_tables(slopes):
    out = np.zeros((len(slopes), BLK, 256), np.float32)
    for h, m in enumerate(slopes):
        rest = (m * LOG2E * np.arange(BLK, dtype=np.float64)).astype(np.float32)
        lo = _bias_lane(h // (len(slopes) // 2))
        for term in range(N_BIAS_TERMS):
            piece = rest.astype(BF16).astype(np.float32)
            out[h, :, lo + term] = piece
            rest = rest - piece
    return jnp.asarray(out.astype(BF16))


def _reorder_w_in(w_in):
    o = np.cumsum([0, 256, 256, 256, 256, 256, 256, 256, 256, 256, 256, 128, 32, 4])
    piece = lambda n: w_in[:, o[n]:o[n + 1]]
    d = w_in.shape[0]
    cols = [piece(n) for n in (0, 1, 3, 4, 7, 8, 2, 5, 9, 6, 10)]
    cols += [piece(11)] * (LANE // IDX_DH)
    cols += [piece(12), jnp.zeros((d, LANE - N_IDX_HEADS), w_in.dtype)]
    return jnp.concatenate(cols, axis=1)


def _q_colscale():
    s = np.ones((1, ATT_W), np.float32)
    s[0, 0:256] = HEAD ** -0.5 * LOG2E
    s[0, 512:768] = DIFF_DH ** -0.5 * LOG2E
    s[0, 1024:1280] = HEAD ** -0.5 * LOG2E
    return jnp.asarray(s)


def _block_diag(w_pool):
    g, c, _ = w_pool.shape
    out = jnp.zeros((g * c, g * c), w_pool.dtype)
    for n in range(g):
        out = out.at[n * c:(n + 1) * c, n * c:(n + 1) * c].set(w_pool[n])
    return out


def kernel(x, c, w_ada, b_ada, norm1_g, w_in, diff_lambda, diff_subln_g, w_pool, pool_scale, w_gate, b_gate,
           w_branch, w_out, norm2_g, w_ff1, w_ff2, final_norm_g):
    b, seq, d = x.shape
    depth = w_ada.shape[0]
    t = b * seq
    tm = 512
    nblk = seq // BLK
    nb_pad = -(-nblk // SUBLANE) * SUBLANE
    assert seq % tm == 0
    mods = _modulation(c, w_ada, b_ada).reshape(depth, b, 6, d)
    bias_a, bias_b, bias_d = _alibi_tables(SLOPES_A), _alibi_tables(SLOPES_B), _alibi_tables(SLOPES_D)
    tri = jnp.asarray(np.tril(np.ones((BLK, BLK), np.float32))).astype(BF16)
    colscale = _q_colscale()
    x2 = x.reshape(t, d)
    for l in range(depth):
        mod = mods[l]
        g1 = norm1_g[l].reshape(1, d)
        att, vt, pu, idx, iwt, kmean = _in_proj(x2, mod, g1, _reorder_w_in(w_in[l]).astype(BF16), colscale,
                                                b, seq, tm)
        att = att.reshape(b, seq, ATT_W)
        kmean_pad = jnp.pad(kmean.reshape(b, nblk, 256), ((0, 0), (0, nb_pad - nblk), (0, 0)))
        y_a = _moba(att, vt, kmean_pad, bias_a, seq)
        lambda_init = 0.8 - 0.6 * math.exp(-0.3 * l)
        y_b = _diff(att, vt, bias_b, diff_lambda[l], jnp.tile(diff_subln_g[l], 2).reshape(1, LANE), lambda_init, seq)
        y_c = _pool(pu.reshape(b, seq, 256), _block_diag(w_pool[l]).astype(BF16), pool_scale[l].reshape(1, 256), tm)
        y_d = _dsa(att, vt, idx.reshape(b, seq, 2 * LANE), iwt, bias_d, tri, seq)
        ys = [y.reshape(t, 256) for y in (y_a, y_b, y_c, y_d)]
        x2 = _merge(x2, mod, g1, ys, w_gate[l].astype(BF16), b_gate[l][:, None, :], w_branch[l].astype(BF16),
                    w_out[l].astype(BF16), seq, tm)
        x2 = _ffn(x2, mod, norm2_g[l].reshape(1, d), w_ff1[l].astype(BF16), w_ff2[l].astype(BF16),
                  final_norm_g.reshape(1, d), l == depth - 1, seq, tm)
    return x2.reshape(b, seq, d)
```

```python
import functools
import math

import numpy as np
import jax
import jax.numpy as jnp
from jax import lax
from jax.experimental import pallas as pl
from jax.experimental.pallas import tpu as pltpu

F32 = jnp.float32
BF16 = jnp.bfloat16
HIGHEST = lax.Precision.HIGHEST

NEG = -1e30
BELOW_NEG = -3.0e38
NORM_EPS = 1e-6
LOG2E = 1.4426950408889634
BLK = 256
LANE = 128
SUBLANE = 8
HEAD = 64
N_HEADS = 4
VT_ROWS = HEAD + 16
DIFF_DH = 32
MOBA_TOPK = 3
DSA_TOPK = 256
IDX_DH = 32
N_IDX_HEADS = 4
POOL_WINDOWS = (2, 4, 8, 16)
POOL_HALO = 16
INT_MIN = -(2 ** 31)
HALF_BITS = 16
HALF_MASK = 2 ** HALF_BITS - 1
INT16_MIN = -(2 ** (HALF_BITS - 1))
PACKED_SUBLANE = 16
VMEM_LIMIT = 56 * 1024 * 1024

_ALL_SLOPES = [float(np.float32(2.0 ** (-8.0 * i / 12))) for i in range(1, 13)]
SLOPES_A = _ALL_SLOPES[0::3]
SLOPES_B = _ALL_SLOPES[1::3]
SLOPES_D = _ALL_SLOPES[2::3]

ATT_W = 6 * 256
V_OFF = ATT_W
V_W = 3 * 256
PU_OFF = V_OFF + V_W
IDX_OFF = PU_OFF + 256
IN_W = IDX_OFF + 3 * LANE


def _key_of_neg():
    bits = int(np.array(NEG, np.float32).view(np.int32))
    return INT_MIN - bits


KEY_NEG = _key_of_neg()


def _nt(a, b, precision=None):
    return lax.dot_general(a, b, (((1,), (1,)), ((), ())), preferred_element_type=F32, precision=precision)


def _modulated_norm(x, g, shift, scale):
    ms = jnp.mean(x * x, axis=-1, keepdims=True)
    y = x * lax.rsqrt(ms + NORM_EPS) * g
    return y * (1.0 + scale) + shift


def _mod_kernel(c_ref, w_ref, b_ref, o_ref):
    c = c_ref[...]
    act = c * jax.nn.sigmoid(c)
    o_ref[0] = jnp.dot(act, w_ref[0], preferred_element_type=F32, precision=HIGHEST) + b_ref[0]


def _modulation(c, w_ada, b_ada):
    depth, d, n = w_ada.shape
    b = c.shape[0]
    tn = n // 4
    return pl.pallas_call(
        _mod_kernel,
        out_shape=jax.ShapeDtypeStruct((depth, b, n), F32),
        grid=(depth, n // tn),
        in_specs=[pl.BlockSpec((b, d), lambda l, j: (0, 0)),
                  pl.BlockSpec((1, d, tn), lambda l, j: (l, 0, j)),
                  pl.BlockSpec((1, 1, tn), lambda l, j: (l, 0, j))],
        out_specs=pl.BlockSpec((1, b, tn), lambda l, j: (l, 0, j)),
        compiler_params=pltpu.CompilerParams(vmem_limit_bytes=VMEM_LIMIT),
        name="adaln_mod",
    )(c, w_ada, b_ada.reshape(depth, 1, n))


def _in_kernel(x_ref, mod_ref, g_ref, w_ref, cs_ref, att_ref, vt_ref, pu_ref, idx_ref, iwt_ref, km_ref):
    mod = mod_ref[0]
    h = _modulated_norm(x_ref[...], g_ref[...], mod[0:1], mod[1:2])
    proj = jnp.dot(h.astype(BF16), w_ref[...], preferred_element_type=F32)
    att_ref[...] = (proj[:, :ATT_W] * cs_ref[...]).astype(BF16)
    pu_ref[...] = proj[:, PU_OFF:PU_OFF + 256]
    idx_ref[...] = proj[:, IDX_OFF:IDX_OFF + 2 * LANE].astype(BF16)
    for r in range(km_ref.shape[0]):
        rows = slice(r * BLK, (r + 1) * BLK)
        km_ref[r] = jnp.mean(proj[rows, 256:512], axis=0, keepdims=True)
        for g in range(3):
            vt = proj[rows, V_OFF + g * 256:V_OFF + (g + 1) * 256].T.astype(BF16)
            for hd in range(N_HEADS):
                base = (g * N_HEADS + hd) * VT_ROWS
                vt_ref[0, r, base:base + HEAD, :] = vt[hd * HEAD:(hd + 1) * HEAD]
                vt_ref[0, r, base + HEAD:base + VT_ROWS, :] = jnp.ones((VT_ROWS - HEAD, BLK), BF16)
        iwt_ref[0, r] = proj[rows, IDX_OFF + 2 * LANE:].T[0:SUBLANE, :]


def _in_proj(x2, mod, g, w_in_r, colscale, b, seq, tm):
    t, d = x2.shape
    nb = tm // BLK
    per_seq = seq // tm
    vt_rows = 3 * N_HEADS * VT_ROWS
    return pl.pallas_call(
        _in_kernel,
        out_shape=(jax.ShapeDtypeStruct((t, ATT_W), BF16),
                   jax.ShapeDtypeStruct((b, seq // BLK, vt_rows, BLK), BF16),
                   jax.ShapeDtypeStruct((t, 256), F32),
                   jax.ShapeDtypeStruct((t, 2 * LANE), BF16),
                   jax.ShapeDtypeStruct((b, seq // BLK, SUBLANE, BLK), F32),
                   jax.ShapeDtypeStruct((t // BLK, 1, 256), F32)),
        grid=(t // tm,),
        in_specs=[pl.BlockSpec((tm, d), lambda i: (i, 0)),
                  pl.BlockSpec((1, 6, d), lambda i: (i // per_seq, 0, 0)),
                  pl.BlockSpec((1, d), lambda i: (0, 0)),
                  pl.BlockSpec((d, IN_W), lambda i: (0, 0)),
                  pl.BlockSpec((1, ATT_W), lambda i: (0, 0))],
        out_specs=(pl.BlockSpec((tm, ATT_W), lambda i: (i, 0)),
                   pl.BlockSpec((1, nb, vt_rows, BLK), lambda i: (i // per_seq, i % per_seq, 0, 0)),
                   pl.BlockSpec((tm, 256), lambda i: (i, 0)),
                   pl.BlockSpec((tm, 2 * LANE), lambda i: (i, 0)),
                   pl.BlockSpec((1, nb, SUBLANE, BLK), lambda i: (i // per_seq, i % per_seq, 0, 0)),
                   pl.BlockSpec((nb, 1, 256), lambda i: (i, 0, 0))),
        compiler_params=pltpu.CompilerParams(vmem_limit_bytes=VMEM_LIMIT),
        name="in_proj",
    )(x2, mod, g, w_in_r, colscale)


N_BIAS_TERMS = 3


def _feature_masks(width):
    lane = lax.broadcasted_iota(jnp.int32, (BLK, 256), 1)
    return [(lane >= lo) & (lane < lo + width) for lo in range(0, 256, width)]


def _bias_lane(group):
    return LANE if group == 0 else 0


def _group_queries(q):
    lane = lax.broadcasted_iota(jnp.int32, (BLK, 256), 1)
    out = []
    for group in range(2):
        lo = _bias_lane(group)
        ones = jnp.where((lane >= lo) & (lane < lo + N_BIAS_TERMS), 1.0, 0.0).astype(q.dtype)
        out.append(jnp.where((lane < LANE) == (group == 0), q, ones))
    return out


def _stacked_scores(k_blk, masks, fills, q_groups):
    half = len(masks) // 2
    tiles = []
    for group in range(2):
        rows = []
        for n in range(group * half, (group + 1) * half):
            fill = jnp.zeros((), k_blk.dtype) if fills is None else fills[n]
            rows.append(jnp.where(masks[n], k_blk, fill))
        s = _nt(jnp.concatenate(rows, axis=0), q_groups[group])
        tiles += [s[n * BLK:(n + 1) * BLK] for n in range(half)]
    return tiles


def _causal_t():
    sl = lax.broadcasted_iota(jnp.int32, (BLK, BLK), 0)
    tl = lax.broadcasted_iota(jnp.int32, (BLK, BLK), 1)
    return tl - sl


def _init_state(m_ref, acc_ref):
    m_ref[...] = jnp.full(m_ref.shape, NEG, F32)
    acc_ref[...] = jnp.zeros(acc_ref.shape, F32)


def _flash_step(s, c, n, vt_h, m_ref, acc_ref, chosen=None):
    s3 = s.reshape(BLK // SUBLANE, SUBLANE, BLK)
    m_blk = jnp.max(s3, axis=0)
    for step in (4, 2, 1):
        m_blk = jnp.maximum(m_blk, pltpu.roll(m_blk, step, 0))
    m_old = m_ref[n]
    m_new = jnp.maximum(m_old, m_blk + c)
    shift = m_new - c
    if chosen is not None:
        m_new = jnp.where(chosen, m_new, m_old)
        shift = jnp.where(chosen, shift, -NEG)
    p = jnp.exp2(s3 - shift[None]).reshape(BLK, BLK)
    alpha = jnp.exp2(m_old - m_new)
    pv = jnp.dot(vt_h, p.astype(BF16), preferred_element_type=F32)
    groups = VT_ROWS // SUBLANE
    acc = acc_ref[n].reshape(groups, SUBLANE, BLK) * alpha[None] + pv.reshape(groups, SUBLANE, BLK)
    acc_ref[n] = acc.reshape(VT_ROWS, BLK)
    m_ref[n] = m_new


def _block_offset(slope, i, j):
    steps = jnp.full((1, 1), i - j, jnp.int32).astype(F32)
    return steps * (-slope * LOG2E * BLK)


class _Held:
    def __init__(self, ref):
        self.ref = ref
        self.vals = [ref[n] for n in range(ref.shape[0])]

    def __getitem__(self, n):
        return self.vals[n]

    def __setitem__(self, n, v):
        self.vals[n] = v

    def flush(self):
        for n, v in enumerate(self.vals):
            self.ref[n] = v


def _for_blocks(n, scores, consume, m_ref, acc_ref, consume_last=None):
    def run(first, count, last=None):
        tiles = [scores(first + u) for u in range(count)]
        m, acc = _Held(m_ref), _Held(acc_ref)
        for u in range(count):
            (last if last is not None and u == count - 1 else consume)(tiles[u], first + u, m, acc)
        m.flush()
        acc.flush()

    def trip(t, carry):
        run(4 * t, 4)
        return carry

    held = 0 if consume_last is None else jnp.minimum(n, 2)
    m = n - held
    lax.fori_loop(0, m // 4, trip, 0)
    rest = m % 4

    @pl.when(rest >= 2)
    def _():
        run(m - rest, 2)

    @pl.when(rest % 2 == 1)
    def _():
        run(m - 1, 1)

    if consume_last is not None:
        @pl.when(n >= 2)
        def _():
            run(n - 2, 2, consume_last)

        @pl.when(n == 1)
        def _():
            run(0, 1, consume_last)


def _normalised(acc_ref, n):
    acc = acc_ref[n]
    return acc[0:HEAD] / acc[HEAD:HEAD + 1]


def _write_heads(o_ref, outs):
    for pair in range(2):
        both = jnp.concatenate([outs[2 * pair], outs[2 * pair + 1]], axis=0)
        o_ref[0, :, pair * LANE:(pair + 1) * LANE] = both.T.astype(o_ref.dtype)


def _attn_specs(seq, q_col, k_col, v_group):
    nblk = seq // BLK
    return [pl.BlockSpec((1, BLK, 256), lambda bi, i: (bi, i, q_col)),
            pl.BlockSpec((1, seq, 256), lambda bi, i: (bi, 0, k_col)),
            pl.BlockSpec((1, nblk, N_HEADS * VT_ROWS, BLK), lambda bi, i: (bi, 0, v_group, 0))]


def _state_scratch(n):
    return [pltpu.VMEM((n, SUBLANE, BLK), F32), pltpu.VMEM((n, VT_ROWS, BLK), F32)]


def _head_rows(vt_blk, h):
    return vt_blk[h * VT_ROWS:(h + 1) * VT_ROWS, :]


def _moba_kernel(q_ref, k_ref, vt_ref, km_ref, fill_ref, o_ref, m_ref, acc_ref, sel_ref):
    i = pl.program_id(1)
    nb = km_ref.shape[1]
    masks = _feature_masks(HEAD)
    row = lax.broadcasted_iota(jnp.int32, (nb, BLK), 0)
    row_f = row.astype(F32)
    causal = _causal_t() >= 0
    _init_state(m_ref, acc_ref)
    q = q_ref[0]
    q_groups = _group_queries(q)

    km = km_ref[0]
    km_head = lax.broadcasted_iota(jnp.int32, (nb, 256), 1) // HEAD
    terms = []
    rest = jnp.concatenate([jnp.where(km_head == h, km, 0.0) for h in range(N_HEADS)], axis=0)
    for _ in range(3):
        part = rest.astype(BF16)
        terms.append(part)
        rest = rest - part.astype(F32)
    gates = _nt(jnp.concatenate(terms, axis=0), q)
    rows = N_HEADS * nb
    gates = gates[0:rows] + gates[rows:2 * rows] + gates[2 * rows:3 * rows]
    for h in range(N_HEADS):
        g = jnp.where(row < i, gates[h * nb:(h + 1) * nb], NEG)
        sel = jnp.zeros((nb, BLK), F32)
        for _ in range(MOBA_TOPK):
            mx = jnp.max(g, axis=0, keepdims=True)
            first = jnp.min(jnp.where(g == mx, row_f, 1e9), axis=0, keepdims=True)
            pick = row_f == first
            sel = jnp.where(pick, 1.0, sel)
            g = jnp.where(pick, BELOW_NEG, g)
        sel_ref[h] = jnp.where(row < i, sel, 0.0)

    def scores(j):
        start = pl.multiple_of(j * BLK, BLK)
        fills = [fill_ref[h] for h in range(N_HEADS)]
        return _stacked_scores(k_ref[0, pl.ds(start, BLK), :], masks, fills, q_groups)

    def consume(s_all, j, m, acc):
        vt_j = vt_ref[0, j]
        for h in range(N_HEADS):
            chosen = sel_ref[h, pl.ds(j, 1), :] > 0.5
            _flash_step(s_all[h], _block_offset(SLOPES_A[h], i, j), h, _head_rows(vt_j, h), m, acc, chosen)

    def consume_own(s_own, j, m, acc):
        vt_own = vt_ref[0, j]
        for h in range(N_HEADS):
            _flash_step(jnp.where(causal, s_own[h], NEG), 0.0, h, _head_rows(vt_own, h), m, acc)

    _for_blocks(i + 1, scores, consume, m_ref, acc_ref, consume_own)
    _write_heads(o_ref, [_normalised(acc_ref, h) for h in range(N_HEADS)])


def _moba(att, vt, kmean_pad, bias, seq):
    b = att.shape[0]
    nb = kmean_pad.shape[1]
    return pl.pallas_call(
        _moba_kernel,
        out_shape=jax.ShapeDtypeStruct((b, seq, 256), BF16),
        grid=(b, seq // BLK),
        in_specs=_attn_specs(seq, 0, 1, 0) + [
            pl.BlockSpec((1, nb, 256), lambda bi, i: (bi, 0, 0)),
            pl.BlockSpec((N_HEADS, BLK, BLK), lambda bi, i: (0, 0, 0))],
        out_specs=pl.BlockSpec((1, BLK, 256), lambda bi, i: (bi, i, 0)),
        scratch_shapes=_state_scratch(N_HEADS) + [pltpu.VMEM((N_HEADS, nb, BLK), F32)],
        compiler_params=pltpu.CompilerParams(vmem_limit_bytes=VMEM_LIMIT),
        name="moba_attn",
    )(att, att, vt, kmean_pad, bias)


def _diff_kernel(lambda_init, q_ref, k_ref, vt_ref, fill_ref, dl_ref, g_ref, o_ref, m_ref, acc_ref):
    i = pl.program_id(1)
    masks = _feature_masks(DIFF_DH)
    causal = _causal_t() >= 0
    _init_state(m_ref, acc_ref)
    q_groups = _group_queries(q_ref[0])

    def scores(j):
        start = pl.multiple_of(j * BLK, BLK)
        fills = [fill_ref[n // 2] for n in range(2 * N_HEADS)]
        return _stacked_scores(k_ref[0, pl.ds(start, BLK), :], masks, fills, q_groups)

    def consume(s_all, j, m, acc, own):
        vt_j = vt_ref[0, j]
        for n in range(2 * N_HEADS):
            h = n // 2
            s = jnp.where(causal, s_all[n], NEG) if own else s_all[n]
            c = 0.0 if own else _block_offset(SLOPES_B[h], i, j)
            _flash_step(s, c, n, _head_rows(vt_j, h), m, acc)

    _for_blocks(i + 1, scores, functools.partial(consume, own=False), m_ref, acc_ref,
                functools.partial(consume, own=True))

    dl = dl_ref[...]
    lam = (jnp.exp(jnp.sum(dl[0:1] * dl[1:2], axis=1, keepdims=True))
           - jnp.exp(jnp.sum(dl[2:3] * dl[3:4], axis=1, keepdims=True)) + lambda_init)
    lane = lax.broadcasted_iota(jnp.int32, (BLK, LANE), 1)
    for pair in range(2):
        both = []
        for h in (2 * pair, 2 * pair + 1):
            both.append(_normalised(acc_ref, 2 * h) - lam * _normalised(acc_ref, 2 * h + 1))
        o = jnp.concatenate(both, axis=0).T
        sq = o * o
        ms = jnp.where(lane < HEAD,
                       jnp.sum(jnp.where(lane < HEAD, sq, 0.0), axis=1, keepdims=True),
                       jnp.sum(jnp.where(lane >= HEAD, sq, 0.0), axis=1, keepdims=True)) * (1.0 / HEAD)
        y = o * lax.rsqrt(ms + NORM_EPS) * g_ref[...] * (1.0 - lambda_init)
        o_ref[0, :, pair * LANE:(pair + 1) * LANE] = y.astype(o_ref.dtype)


def _diff(att, vt, bias, diff_lambda, subln_g2, lambda_init, seq):
    b = att.shape[0]
    return pl.pallas_call(
        functools.partial(_diff_kernel, lambda_init),
        out_shape=jax.ShapeDtypeStruct((b, seq, 256), BF16),
        grid=(b, seq // BLK),
        in_specs=_attn_specs(seq, 2, 3, 1) + [
            pl.BlockSpec((N_HEADS, BLK, BLK), lambda bi, i: (0, 0, 0)),
            pl.BlockSpec((4, DIFF_DH), lambda bi, i: (0, 0)),
            pl.BlockSpec((1, LANE), lambda bi, i: (0, 0))],
        out_specs=pl.BlockSpec((1, BLK, 256), lambda bi, i: (bi, i, 0)),
        scratch_shapes=_state_scratch(2 * N_HEADS),
        compiler_params=pltpu.CompilerParams(vmem_limit_bytes=VMEM_LIMIT),
        name="diff_attn",
    )(att, att, vt, bias, diff_lambda, subln_g2)


def _dsa_kernel(topk, q_ref, k_ref, vt_ref, iq_ref, ik_ref, iwt_ref, fill_ref, tri_ref, o_ref,
                m_ref, acc_ref, key_ref, hi_ref, lo_ref):
    i = pl.program_id(1)
    dist = _causal_t()
    _init_state(m_ref, acc_ref)

    iq = iq_ref[0]
    lane = lax.broadcasted_iota(jnp.int32, (BLK, LANE), 1)
    idx_masks = [(lane >= h * IDX_DH) & (lane < (h + 1) * IDX_DH) for h in range(N_IDX_HEADS)]
    iwt = iwt_ref[0, 0]
    wrow = [iwt[h:h + 1, :] * (N_IDX_HEADS ** -0.5 * IDX_DH ** -0.5) for h in range(N_IDX_HEADS)]

    def score_block(j, own):
        start = pl.multiple_of(j * BLK, BLK)
        rel_all = _stacked_scores(ik_ref[0, pl.ds(start, BLK), :], idx_masks, None, (iq, iq))
        sc = jnp.maximum(rel_all[0], 0.0) * wrow[0]
        for h in range(1, N_IDX_HEADS):
            sc = sc + jnp.maximum(rel_all[h], 0.0) * wrow[h]
        if own:
            sc = jnp.where(dist >= 0, sc, NEG)
        bits = lax.bitcast_convert_type(sc, jnp.int32)
        key = jnp.where(bits < 0, INT_MIN - bits, bits)
        key_ref[j] = key
        hi_ref[j] = lax.shift_right_arithmetic(key, HALF_BITS).astype(jnp.int16)
        lo_ref[j] = (jnp.bitwise_and(key, HALF_MASK) + INT16_MIN).astype(jnp.int16)

    def past_pair(t, carry):
        score_block(2 * t, False)
        score_block(2 * t + 1, False)
        return carry

    lax.fori_loop(0, i // 2, past_pair, 0)

    @pl.when(i % 2 == 1)
    def _():
        score_block(i - 1, False)

    score_block(i, True)

    def count_where(ref, pred):
        rows = BLK // PACKED_SUBLANE

        def blk(j, acc):
            hit = jnp.where(pred(ref[j]), jnp.int16(1), jnp.int16(0))
            groups = hit.reshape(rows, PACKED_SUBLANE, BLK)
            parts = [groups[r] for r in range(rows)]
            while len(parts) > 1:
                parts = [a + b for a, b in zip(parts[0::2], parts[1::2])]
            return acc + parts[0].astype(jnp.int32)

        def pair(t, acc):
            return blk(2 * t + 1, blk(2 * t, acc))

        acc = lax.fori_loop(0, (i + 1) // 2, pair, jnp.zeros((PACKED_SUBLANE, BLK), jnp.int32))
        acc = lax.cond(i % 2 == 0, lambda a: blk(i, a), lambda a: a, acc)
        return jnp.sum(acc, axis=0, keepdims=True).astype(F32)

    def half_search(ref, base):
        def step(n, carry):
            ans, above = carry
            cand = ans + jnp.left_shift(jnp.int32(1), HALF_BITS - 1 - n)
            cand16 = cand.astype(jnp.int16)
            cnt = base + count_where(ref, lambda half: half >= cand16)
            ok = cnt >= topk
            return jnp.where(ok, cand, ans), jnp.where(ok, above, cnt)

        start = jnp.full((1, BLK), INT16_MIN, jnp.int32)
        return lax.fori_loop(0, HALF_BITS, step, (start, base + jnp.zeros((1, BLK), F32)))

    thr_hi, above_hi = half_search(hi_ref, 0.0)
    thr_hi16 = thr_hi.astype(jnp.int16)

    def keep_bucket(j, carry):
        lo_ref[j] = jnp.where(hi_ref[j] == thr_hi16, lo_ref[j], jnp.int16(INT16_MIN))
        return carry

    lax.fori_loop(0, i + 1, keep_bucket, 0)
    thr_lo, n_above = half_search(lo_ref, above_hi)
    thr =jnp.left_shift(thr_hi, HALF_BITS) + (thr_lo - INT16_MIN)
    need = jnp.where(thr == KEY_NEG, 0.0, topk - n_above)

    def tie_rank(j):
        key = key_ref[j]
        tied = key == thr
        tie = jnp.where(tied, 1.0, 0.0)
        return key, tied, tie, jnp.dot(tri_ref[...], tie.astype(BF16), preferred_element_type=F32)

    def mark(j, parts, taken):
        key, tied, tie, rank = parts
        tie_bias = jnp.where(rank <= need - taken, 0.0, NEG)
        bias = jnp.where(key > thr, 0.0, jnp.where(tied, tie_bias, NEG))
        key_ref[j] = lax.bitcast_convert_type(bias, jnp.int32)
        return taken + jnp.sum(tie, axis=0, keepdims=True)

    def mark_pair(t, taken):
        first, second = tie_rank(2 * t), tie_rank(2 * t + 1)
        return mark(2 * t + 1, second, mark(2 * t, first, taken))

    taken = lax.fori_loop(0, (i + 1) // 2, mark_pair, jnp.zeros((1, BLK), F32))
    lax.cond(i % 2 == 0, lambda tk: mark(i, tie_rank(i), tk), lambda tk: tk, taken)

    q_groups = _group_queries(q_ref[0])
    masks = _feature_masks(HEAD)

    def scores(j):
        start = pl.multiple_of(j * BLK, BLK)
        fills = [fill_ref[h] for h in range(N_HEADS)]
        return _stacked_scores(k_ref[0, pl.ds(start, BLK), :], masks, fills, q_groups)

    def attend(s_all, j, m, acc):
        vt_j = vt_ref[0, j]
        bias = lax.bitcast_convert_type(key_ref[j], F32)
        for h in range(N_HEADS):
            _flash_step(s_all[h] + bias, _block_offset(SLOPES_D[h], i, j), h, _head_rows(vt_j, h), m, acc)

    _for_blocks(i + 1, scores, attend, m_ref, acc_ref)
    _write_heads(o_ref, [_normalised(acc_ref, h) for h in range(N_HEADS)])


def _dsa(att, vt, idx, iwt, bias, tri, seq):
    b = att.shape[0]
    nblk = seq // BLK
    topk = float(min(DSA_TOPK, seq // 4))
    return pl.pallas_call(
        functools.partial(_dsa_kernel, topk),
        out_shape=jax.ShapeDtypeStruct((b, seq, 256), BF16),
        grid=(b, nblk),
        in_specs=_attn_specs(seq, 4, 5, 2) + [
            pl.BlockSpec((1, BLK, LANE), lambda bi, i: (bi, i, 0)),
            pl.BlockSpec((1, seq, LANE), lambda bi, i: (bi, 0, 1)),
            pl.BlockSpec((1, 1, SUBLANE, BLK), lambda bi, i: (bi, i, 0, 0)),
            pl.BlockSpec((N_HEADS, BLK, BLK), lambda bi, i: (0, 0, 0)),
            pl.BlockSpec((BLK, BLK), lambda bi, i: (0, 0))],
        out_specs=pl.BlockSpec((1, BLK, 256), lambda bi, i: (bi, i, 0)),
        scratch_shapes=_state_scratch(N_HEADS) + [pltpu.VMEM((nblk, BLK, BLK), jnp.int32),
                                                  pltpu.VMEM((nblk, BLK, BLK), jnp.int16),
                                                  pltpu.VMEM((nblk, BLK, BLK), jnp.int16)],
        compiler_params=pltpu.CompilerParams(vmem_limit_bytes=VMEM_LIMIT),
        name="dsa_attn",
    )(att, att, vt, idx, idx, iwt, bias, tri)


def _pool_kernel(tp, u_ref, halo_ref, w_ref, ps_ref, o_ref, a_ref, b_ref):
    i = pl.program_id(1)
    u = u_ref[0]
    body = POOL_HALO + tp
    a_ref[0:POOL_HALO] = jnp.zeros((POOL_HALO, 256), F32)
    b_ref[0:POOL_HALO] = jnp.zeros((POOL_HALO, 256), F32)
    a_ref[POOL_HALO:2 * POOL_HALO] = jnp.where(i > 0, halo_ref[0], 0.0)
    a_ref[2 * POOL_HALO:] = u
    group = jnp.right_shift(lax.broadcasted_iota(jnp.int32, (tp, 256), 1), 6)
    src, dst = a_ref, b_ref
    win = jnp.zeros((tp, 256), F32)
    shift = 1
    for g, w in enumerate(POOL_WINDOWS):
        while shift < w:
            dst[POOL_HALO:] = src[POOL_HALO:] + src[POOL_HALO - shift:POOL_HALO - shift + body]
            src, dst = dst, src
            shift *= 2
        win = jnp.where(group == g, src[2 * POOL_HALO:], win)
    t = i * tp + lax.broadcasted_iota(jnp.int32, (tp, 256), 0)
    width = jnp.where(group == 0, 2, jnp.where(group == 1, 4, jnp.where(group == 2, 8, 16)))
    cnt = jnp.minimum(t + 1, width).astype(F32)
    d = win / cnt - u
    y = jnp.dot(d.astype(BF16), w_ref[...], preferred_element_type=F32) * ps_ref[...]
    o_ref[0] = y.astype(o_ref.dtype)


def _pool(pu, w_pool_bd, pool_scale, tp):
    b, seq, _ = pu.shape
    per = tp // POOL_HALO
    return pl.pallas_call(
        functools.partial(_pool_kernel, tp),
        out_shape=jax.ShapeDtypeStruct((b, seq, 256), BF16),
        grid=(b, seq // tp),
        in_specs=[pl.BlockSpec((1, tp, 256), lambda bi, i: (bi, i, 0)),
                  pl.BlockSpec((1, POOL_HALO, 256), lambda bi, i: (bi, jnp.maximum(i * per - 1, 0), 0)),
                  pl.BlockSpec((256, 256), lambda bi, i: (0, 0)),
                  pl.BlockSpec((1, 256), lambda bi, i: (0, 0))],
        out_specs=pl.BlockSpec((1, tp, 256), lambda bi, i: (bi, i, 0)),
        scratch_shapes=[pltpu.VMEM((tp + 2 * POOL_HALO, 256), F32),
                        pltpu.VMEM((tp + 2 * POOL_HALO, 256), F32)],
        compiler_params=pltpu.CompilerParams(vmem_limit_bytes=VMEM_LIMIT),
        name="pool_mixer",
    )(pu, pu, w_pool_bd, pool_scale)


def _merge_kernel(x_ref, mod_ref, g_ref, ya_ref, yb_ref, yc_ref, yd_ref, wg_ref, bg_ref, wb_ref, wo_ref, o_ref):
    mod = mod_ref[0]
    x = x_ref[...]
    hb = _modulated_norm(x, g_ref[...], mod[0:1], mod[1:2]).astype(BF16)
    merged = jnp.zeros(x.shape, F32)
    for n, y_ref in enumerate((ya_ref, yb_ref, yc_ref, yd_ref)):
        gate = jax.nn.sigmoid(jnp.dot(hb, wg_ref[n], preferred_element_type=F32) + bg_ref[n])
        merged = merged + gate * jnp.dot(y_ref[...], wb_ref[n], preferred_element_type=F32)
    o_ref[...] = x + mod[2:3] * jnp.dot(merged.astype(BF16), wo_ref[...], preferred_element_type=F32)


def _resident(shape):
    zeros = (0,) * len(shape)
    return pl.BlockSpec(shape, lambda i: zeros, pipeline_mode=pl.Buffered(1))


def _merge(x2, mod, g, ys, w_gate, b_gate, w_branch, w_out, seq, tm):
    t, d = x2.shape
    tok = lambda w: pl.BlockSpec((tm, w), lambda i: (i, 0))
    return pl.pallas_call(
        _merge_kernel,
        out_shape=jax.ShapeDtypeStruct((t, d), F32),
        grid=(t // tm,),
        in_specs=[tok(d),
                  pl.BlockSpec((1, 6, d), lambda i: (i * tm // seq, 0, 0)),
                  _resident((1, d)),
                  tok(256), tok(256), tok(256), tok(256),
                  _resident(w_gate.shape), _resident(b_gate.shape), _resident(w_branch.shape),
                  _resident(w_out.shape)],
        out_specs=tok(d),
        compiler_params=pltpu.CompilerParams(vmem_limit_bytes=VMEM_LIMIT),
        name="gated_merge",
    )(x2, mod, g, *ys, w_gate, b_gate, w_branch, w_out)


def _ffn_kernel(final, x_ref, mod_ref, g_ref, w1_ref, w2_ref, fg_ref, o_ref):
    mod = mod_ref[0]
    x = x_ref[...]
    hb = _modulated_norm(x, g_ref[...], mod[3:4], mod[4:5]).astype(BF16)
    d = x.shape[1]
    ff = jnp.zeros(x.shape, F32)
    for c in range(w1_ref.shape[1] // d):
        a = jnp.maximum(jnp.dot(hb, w1_ref[:, c * d:(c + 1) * d], preferred_element_type=F32), 0.0)
        ff = ff + jnp.dot((a * a).astype(BF16), w2_ref[c * d:(c + 1) * d, :], preferred_element_type=F32)
    out = x + mod[5:6] * ff
    if final:
        ms = jnp.mean(out * out, axis=-1, keepdims=True)
        out = out * lax.rsqrt(ms + NORM_EPS) * fg_ref[...]
    o_ref[...] = out


def _ffn(x2, mod, g, w1, w2, final_g, final, seq, tm):
    t, d = x2.shape
    tok = pl.BlockSpec((tm, d), lambda i: (i, 0))
    return pl.pallas_call(
        functools.partial(_ffn_kernel, final),
        out_shape=jax.ShapeDtypeStruct((t, d), F32),
        grid=(t // tm,),
        in_specs=[tok,
                  pl.BlockSpec((1, 6, d), lambda i: (i * tm // seq, 0, 0)),
                  _resident((1, d)), _resident(w1.shape), _resident(w2.shape), _resident((1, d))],
        out_specs=tok,
        compiler_params=pltpu.CompilerParams(vmem_limit_bytes=VMEM_LIMIT),
        name="ffn",
    )(x2, mod, g, w1, w2, final_g)


def _alibi_tables(slopes):
    out = np.zeros((len(slopes), BLK, 256), np.float32)
    for h, m in enumerate(slopes):
        rest = (m * LOG2E * np.arange(BLK, dtype=np.float64)).astype(np.float32)
        lo = _bias_lane(h // (len(slopes) // 2))
        for term in range(N_BIAS_TERMS):
            piece = rest.astype(BF16).astype(np.float32)
            out[h, :, lo + term] = piece
            rest = rest - piece
    return jnp.asarray(out.astype(BF16))


def _reorder_w_in(w_in):
    o = np.cumsum([0, 256, 256, 256, 256, 256, 256, 256, 256, 256, 256, 128, 32, 4])
    piece = lambda n: w_in[:, o[n]:o[n + 1]]
    d = w_in.shape[0]
    cols = [piece(n) for n in (0, 1, 3, 4, 7, 8, 2, 5, 9, 6, 10)]
    cols += [piece(11)] * (LANE // IDX_DH)
    cols += [piece(12), jnp.zeros((d, LANE - N_IDX_HEADS), w_in.dtype)]
    return jnp.concatenate(cols, axis=1)


def _q_colscale():
    s = np.ones((1, ATT_W), np.float32)
    s[0, 0:256] = HEAD ** -0.5 * LOG2E
    s[0, 512:768] = DIFF_DH ** -0.5 * LOG2E
    s[0, 1024:1280] = HEAD ** -0.5 * LOG2E
    return jnp.asarray(s)


def _block_diag(w_pool):
    g, c, _ = w_pool.shape
    out = jnp.zeros((g * c, g * c), w_pool.dtype)
    for n in range(g):
        out = out.at[n * c:(n + 1) * c, n * c:(n + 1) * c].set(w_pool[n])
    return out


def kernel(x, c, w_ada, b_ada, norm1_g, w_in, diff_lambda, diff_subln_g, w_pool, pool_scale, w_gate, b_gate,
           w_branch, w_out, norm2_g, w_ff1, w_ff2, final_norm_g):
    b, seq, d = x.shape
    depth = w_ada.shape[0]
    t = b * seq
    tm = 512
    nblk = seq // BLK
    nb_pad = -(-nblk // SUBLANE) * SUBLANE
    assert seq % tm == 0
    mods = _modulation(c, w_ada, b_ada).reshape(depth, b, 6, d)
    bias_a, bias_b, bias_d = _alibi_tables(SLOPES_A), _alibi_tables(SLOPES_B), _alibi_tables(SLOPES_D)
    tri = jnp.asarray(np.tril(np.ones((BLK, BLK), np.float32))).astype(BF16)
    colscale = _q_colscale()
    x2 = x.reshape(t, d)
    for l in range(depth):
        mod = mods[l]
        g1 = norm1_g[l].reshape(1, d)
        att, vt, pu, idx, iwt, kmean = _in_proj(x2, mod, g1, _reorder_w_in(w_in[l]).astype(BF16), colscale,
                                                b, seq, tm)
        att = att.reshape(b, seq, ATT_W)
        kmean_pad = jnp.pad(kmean.reshape(b, nblk, 256), ((0, 0), (0, nb_pad - nblk), (0, 0)))
        y_a = _moba(att, vt, kmean_pad, bias_a, seq)
        lambda_init = 0.8 - 0.6 * math.exp(-0.3 * l)
        y_b = _diff(att, vt, bias_b, diff_lambda[l], jnp.tile(diff_subln_g[l], 2).reshape(1, LANE), lambda_init, seq)
        y_c = _pool(pu.reshape(b, seq, 256), _block_diag(w_pool[l]).astype(BF16), pool_scale[l].reshape(1, 256), tm)
        y_d = _dsa(att, vt, idx.reshape(b, seq, 2 * LANE), iwt, bias_d, tri, seq)
        ys = [y.reshape(t, 256) for y in (y_a, y_b, y_c, y_d)]
        x2 = _merge(x2, mod, g1, ys, w_gate[l].astype(BF16), b_gate[l][:, None, :], w_branch[l].astype(BF16),
                    w_out[l].astype(BF16), seq, tm)
        x2 = _ffn(x2, mod, norm2_g[l].reshape(1, d), w_ff1[l].astype(BF16), w_ff2[l].astype(BF16),
                  final_norm_g.reshape(1, d), l == depth - 1, seq, tm)
    return x2.reshape(b, seq, d)
```
